```python
import math
import jax, jax.numpy as jnp
from jax import lax
import numpy as np

D_MODEL = 1024
BATCH = 4
SEQ = 4096
DEPTH = 2
DEC_BATCH = 32
DEC_SEQ = 8
PAST_LEN = 16384
PAGE_SIZE = 128

N_MIXERS = 2
N_CONV_LAYERS = (DEPTH + 1) // 2
N_NSA_LAYERS = DEPTH // 2
EXPAND = 2
D_IN = EXPAND * D_MODEL
CONV_W = 31
N_HEADS = 16
HEAD_DIM = D_MODEL // N_HEADS
N_KV = 4
GROUP = N_HEADS // N_KV
ATT_W = N_HEADS * HEAD_DIM
KV_W = N_KV * HEAD_DIM
N_KV_STREAMS = 6
NSA_IN = 2 * ATT_W + N_KV_STREAMS * KV_W + 3 * N_HEADS
CMP_BLOCK = 32
CMP_STRIDE = 16
CMP_HID = 2 * HEAD_DIM
SEL_BLOCK = 64
N_SELECT = 16
WINDOW = 512
Q_BLOCK = 64
ALPHA = (2 * DEPTH) ** 0.25
BETA = (8 * DEPTH) ** -0.25
LN_EPS = 1e-5

kernel_name = "nsa_conformer_hybrid_step"


def layer_norm(x, g, b):
    xf = x.astype(jnp.float32)
    mu = jnp.mean(xf, axis=-1, keepdims=True)
    var = jnp.mean(jnp.square(xf - mu), axis=-1, keepdims=True)
    return ((xf - mu) * lax.rsqrt(var + LN_EPS)).astype(x.dtype) * g + b


def masked_softmax(s, mask):
    s = jnp.where(mask, s, -jnp.inf)
    m = jnp.max(s, axis=-1, keepdims=True)
    m = jnp.where(jnp.isfinite(m), m, 0.0)
    e = jnp.where(mask, jnp.exp(s - m), 0.0)
    return e / jnp.maximum(jnp.sum(e, axis=-1, keepdims=True), 1e-30)


def conv_mixer(x, state, w_in, b_in, dw_w, dw_b, ln_g, ln_b, w_out):
    h = x @ w_in + b_in
    a, gl, z = jnp.split(h, 3, axis=-1)
    u = a * jax.nn.sigmoid(gl)
    if state is None:
        up = jnp.pad(u, ((0, 0), (CONV_W - 1, 0), (0, 0)))
    else:
        up = jnp.concatenate([state, u], axis=1)
    new_state = up[:, up.shape[1] - (CONV_W - 1):]
    y = lax.conv_general_dilated(up, dw_w[:, None, :], window_strides=(1,), padding="VALID",
                                 dimension_numbers=("NWC", "WIO", "NWC"),
                                 feature_group_count=D_IN) + dw_b
    y = layer_norm(y, ln_g, ln_b)
    y = jax.nn.silu(y) * jax.nn.silu(z)
    return y @ w_out, new_state


def compress(kseq, pe, w1, b1, w2):
    B, L = kseq.shape[:2]
    r = CMP_BLOCK // CMP_STRIDE
    nchunk = L // CMP_STRIDE
    chunks = kseq[:, :nchunk * CMP_STRIDE].reshape(B, nchunk, CMP_STRIDE, N_KV, HEAD_DIM)
    w1r = w1.reshape(r, CMP_STRIDE, HEAD_DIM, CMP_HID)
    parts = jnp.einsum("bcsgd,jsdh->jbcgh", chunks, w1r)
    nc = nchunk - r + 1
    h = jnp.einsum("sd,sdh->h", pe, w1) + b1
    for j in range(r):
        h = h + parts[j][:, j:j + nc]
    return jnp.einsum("bcgh,hd->bcgd", jax.nn.silu(h), w2)


def cmp_to_sel(imp, n_sel):
    r = SEL_BLOCK // CMP_STRIDE
    lo = -(CMP_BLOCK // CMP_STRIDE - 1)
    nc = imp.shape[-1]
    p = jnp.pad(imp, [(0, 0)] * (imp.ndim - 1) + [(-lo, r * n_sel - nc)])
    out = 0.0
    for o in range(lo, r):
        start = o * CMP_STRIDE
        ov = max(0, min(SEL_BLOCK, start + CMP_BLOCK) - max(0, start))
        if ov == 0:
            continue
        out = out + (ov / CMP_BLOCK) * p[..., o - lo:o - lo + r * (n_sel - 1) + 1:r]
    return out


def nsa_attend(q, gates, pos_q, kc, vc, ksb, vsb, kw, vw, pos_w):
    B, Q = q.shape[:2]
    scale = HEAD_DIM ** -0.5
    t = pos_q[:, None]
    nc = kc.shape[1]
    cmp_end = jnp.arange(nc, dtype=jnp.int32) * CMP_STRIDE + CMP_BLOCK - 1
    m_cmp = (cmp_end[None, :] <= t)[None, :, None, None, :]
    s = jnp.einsum("bqgrd,bngd->bqgrn", q, kc).astype(jnp.float32) * scale
    p_cmp = masked_softmax(s, m_cmp)
    o_cmp = jnp.einsum("bqgrn,bngd->bqgrd", p_cmp.astype(vc.dtype), vc)
    n_sel = ksb.shape[2]
    imp = cmp_to_sel(jnp.sum(p_cmp, axis=3), n_sel)
    blk = jnp.arange(n_sel, dtype=jnp.int32)[None, :]
    cur = (pos_q // SEL_BLOCK)[:, None]
    causal = (blk <= cur)[None, :, None, :]
    forced = ((blk == 0) | (blk == cur) | (blk == cur - 1))[None, :, None, :]
    score = jnp.where(forced, jnp.inf, jnp.where(causal, imp, -jnp.inf))
    k_top = min(N_SELECT, n_sel)
    vals, idx = lax.top_k(score, k_top)
    blk_ok = vals > -jnp.inf
    bi = jnp.arange(B)[:, None, None, None]
    gi = jnp.arange(N_KV)[None, None, :, None]
    kg = ksb[bi, gi, idx]
    vg = vsb[bi, gi, idx]
    tok = idx[..., None] * SEL_BLOCK + jnp.arange(SEL_BLOCK, dtype=jnp.int32)
    m_sel = blk_ok[..., None] & (tok <= pos_q[None, :, None, None, None])
    m_sel = m_sel.reshape(B, Q, N_KV, 1, k_top * SEL_BLOCK)
    s = jnp.einsum("bqgrd,bqgksd->bqgrks", q, kg).astype(jnp.float32) * scale
    p = masked_softmax(s.reshape(B, Q, N_KV, GROUP, k_top * SEL_BLOCK), m_sel)
    p = p.reshape(B, Q, N_KV, GROUP, k_top, SEL_BLOCK).astype(vg.dtype)
    o_sel = jnp.einsum("bqgrks,bqgksd->bqgrd", p, vg)
    dpos = t - pos_w[None, :]
    m_win = ((dpos >= 0) & (dpos < WINDOW) & (pos_w[None, :] >= 0))[None, :, None, None, :]
    s = jnp.einsum("bqgrd,bsgd->bqgrs", q, kw).astype(jnp.float32) * scale
    p = masked_softmax(s, m_win)
    o_win = jnp.einsum("bqgrs,bsgd->bqgrd", p.astype(vw.dtype), vw)
    return gates[..., 0:1] * o_cmp + gates[..., 1:2] * o_sel + gates[..., 2:3] * o_win


def nsa_project(x, w_in):
    B, T, _ = x.shape
    h = x @ w_in
    o1 = ATT_W
    o2 = o1 + N_KV_STREAMS * KV_W
    o3 = o2 + 3 * N_HEADS
    q = h[..., :o1].reshape(B, T, N_KV, GROUP, HEAD_DIM)
    kv = h[..., o1:o2].reshape(B, T, N_KV_STREAMS, N_KV, HEAD_DIM)
    gates = jax.nn.sigmoid(h[..., o2:o3].reshape(B, T, N_KV, GROUP, 3))
    z = h[..., o3:]
    return q, kv, gates, z


def branch_keys(seq, pe, w1, b1, w2):
    B, L = seq.shape[:2]
    kc = compress(seq[:, :, 0], pe[0], w1[0], b1[0], w2[0])
    vc = compress(seq[:, :, 1], pe[1], w1[1], b1[1], w2[1])
    sel = seq[:, :, 2:].reshape(B, L // SEL_BLOCK, SEL_BLOCK, 2, N_KV, HEAD_DIM)
    sel = sel.transpose(3, 0, 4, 1, 2, 5)
    return kc, vc, sel[0], sel[1]


def nsa_output(o, z, w_out):
    B, T = o.shape[:2]
    return (o.reshape(B, T, ATT_W) * jax.nn.silu(z)) @ w_out


def nsa_prompt(x, w_in, pe, w1, b1, w2, w_out):
    B, T, _ = x.shape
    q, kv, gates, z = nsa_project(x, w_in)
    rows, win = kv[:, :, :4], kv[:, :, 4:]
    kc, vc, ksb, vsb = branch_keys(rows, pe, w1, b1, w2)
    wpad = jnp.pad(win, ((0, 0), (WINDOW, 0), (0, 0), (0, 0), (0, 0)))

    def block(i):
        qs = i * Q_BLOCK
        qb = lax.dynamic_slice_in_dim(q, qs, Q_BLOCK, axis=1)
        gb = lax.dynamic_slice_in_dim(gates, qs, Q_BLOCK, axis=1)
        wb = lax.dynamic_slice_in_dim(wpad, qs, WINDOW + Q_BLOCK, axis=1)
        pos_q = qs + jnp.arange(Q_BLOCK, dtype=jnp.int32)
        pos_w = qs - WINDOW + jnp.arange(WINDOW + Q_BLOCK, dtype=jnp.int32)
        return nsa_attend(qb, gb, pos_q, kc, vc, ksb, vsb, wb[:, :, 0], wb[:, :, 1], pos_w)

    o = lax.map(block, jnp.arange(T // Q_BLOCK, dtype=jnp.int32))
    o = jnp.moveaxis(o, 0, 1).reshape(B, T, N_KV, GROUP, HEAD_DIM)
    return nsa_output(o, z, w_out), rows, win[:, T - min(WINDOW, T):]


def nsa_sample(x, cache_kv_l, cache_win_l, page_table, w_in, pe, w1, b1, w2, w_out):
    B, T, _ = x.shape
    q, kv, gates, z = nsa_project(x, w_in)
    rows, win = kv[:, :, :4], kv[:, :, 4:]
    past_len = page_table.shape[1] * cache_kv_l.shape[1]
    past = cache_kv_l[page_table].reshape(B, past_len, 4, N_KV, HEAD_DIM)
    n_pad = (-T) % SEL_BLOCK
    seq = jnp.concatenate([past, rows, jnp.zeros((B, n_pad) + rows.shape[2:], rows.dtype)], axis=1)
    kc, vc, ksb, vsb = branch_keys(seq, pe, w1, b1, w2)
    wbuf = cache_win_l.shape[1]
    wseq = jnp.concatenate([cache_win_l, win], axis=1)
    pos_w = past_len - wbuf + jnp.arange(wbuf + T, dtype=jnp.int32)
    pos_q = past_len + jnp.arange(T, dtype=jnp.int32)
    o = nsa_attend(q, gates, pos_q, kc, vc, ksb, vsb, wseq[:, :, 0], wseq[:, :, 1], pos_w)
    return nsa_output(o, z, w_out), rows, wseq[:, T:]


def setup_inputs(seed: int = 0) -> dict:
    key = jax.random.key(seed)
    ks = jax.random.split(key, 24)
    n_pages = PAST_LEN // PAGE_SIZE
    n_used = DEC_BATCH * n_pages
    n_pool = n_used + n_used // 4
    win_buf = min(WINDOW, PAST_LEN)
    f32 = jnp.float32

    def nrm(k, shape, s):
        return jax.random.normal(k, shape, f32) * s

    page_table = jax.random.permutation(ks[5], n_pool)[:n_used].reshape(DEC_BATCH, n_pages).astype(jnp.int32)
    return {
        "x_prompt": nrm(ks[0], (BATCH, SEQ, D_MODEL), 1.0),
        "x_sample": nrm(ks[1], (DEC_BATCH, DEC_SEQ, D_MODEL), 1.0),
        "state_conv": nrm(ks[2], (N_CONV_LAYERS, DEC_BATCH, CONV_W - 1, D_IN), 0.5),
        "cache_kv": nrm(ks[3], (N_NSA_LAYERS, n_pool, PAGE_SIZE, 4, N_KV, HEAD_DIM), 1.0),
        "cache_win": nrm(ks[4], (N_NSA_LAYERS, DEC_BATCH, win_buf, 2, N_KV, HEAD_DIM), 1.0),
        "page_table": page_table,
        "conv_w_in": nrm(ks[6], (N_CONV_LAYERS, D_MODEL, 3 * D_IN), D_MODEL ** -0.5),
        "conv_b_in": nrm(ks[7], (N_CONV_LAYERS, 3 * D_IN), 0.02),
        "conv_dw_w": nrm(ks[8], (N_CONV_LAYERS, CONV_W, D_IN), CONV_W ** -0.5),
        "conv_dw_b": nrm(ks[9], (N_CONV_LAYERS, D_IN), 0.02),
        "conv_ln_g": 1.0 + nrm(ks[10], (N_CONV_LAYERS, D_IN), 0.02),
        "conv_ln_b": nrm(ks[11], (N_CONV_LAYERS, D_IN), 0.02),
        "conv_w_out": nrm(ks[12], (N_CONV_LAYERS, D_IN, D_MODEL), BETA * D_IN ** -0.5),
        "nsa_w_in": nrm(ks[13], (N_NSA_LAYERS, D_MODEL, NSA_IN), D_MODEL ** -0.5),
        "cmp_pe": nrm(ks[14], (N_NSA_LAYERS, 2, CMP_BLOCK, HEAD_DIM), 0.1),
        "cmp_w1": nrm(ks[15], (N_NSA_LAYERS, 2, CMP_BLOCK, HEAD_DIM, CMP_HID), (CMP_BLOCK * HEAD_DIM) ** -0.5),
        "cmp_b1": nrm(ks[16], (N_NSA_LAYERS, 2, CMP_HID), 0.02),
        "cmp_w2": nrm(ks[17], (N_NSA_LAYERS, 2, CMP_HID, HEAD_DIM), CMP_HID ** -0.5),
        "nsa_w_out": nrm(ks[18], (N_NSA_LAYERS, ATT_W, D_MODEL), BETA * ATT_W ** -0.5),
        "ln_g": 1.0 + nrm(ks[19], (DEPTH, D_MODEL), 0.02),
        "ln_b": nrm(ks[20], (DEPTH, D_MODEL), 0.02),
    }


def reference(x_prompt, x_sample, state_conv, cache_kv, cache_win, page_table,
              conv_w_in, conv_b_in, conv_dw_w, conv_dw_b, conv_ln_g, conv_ln_b, conv_w_out,
              nsa_w_in, cmp_pe, cmp_w1, cmp_b1, cmp_w2, nsa_w_out, ln_g, ln_b):
    xp, xs = x_prompt, x_sample
    conv_p, conv_s, kv_p, kv_s, win_p, win_s = [], [], [], [], [], []
    for i in range(DEPTH):
        j = i // N_MIXERS
        if i % N_MIXERS == 0:
            cw = (conv_w_in[j], conv_b_in[j], conv_dw_w[j], conv_dw_b[j],
                  conv_ln_g[j], conv_ln_b[j], conv_w_out[j])
            yp, sp = conv_mixer(xp, None, *cw)
            ys, ss = conv_mixer(xs, state_conv[j], *cw)
            conv_p.append(sp)
            conv_s.append(ss)
        else:
            nw = (nsa_w_in[j], cmp_pe[j], cmp_w1[j], cmp_b1[j], cmp_w2[j], nsa_w_out[j])
            yp, rp, wp = nsa_prompt(xp, *nw)
            ys, rs, ws = nsa_sample(xs, cache_kv[j], cache_win[j], page_table, *nw)
            kv_p.append(rp)
            kv_s.append(rs)
            win_p.append(wp)
            win_s.append(ws)
        xp = layer_norm(ALPHA * xp + yp, ln_g[i], ln_b[i])
        xs = layer_norm(ALPHA * xs + ys, ln_g[i], ln_b[i])
    return (xp, xs, jnp.stack(conv_p), jnp.stack(conv_s), jnp.stack(kv_p), jnp.stack(kv_s),
            jnp.stack(win_p), jnp.stack(win_s))
```

```python
import functools

import jax
import jax.numpy as jnp
import numpy as np
from jax import lax
from jax.experimental import pallas as pl
from jax.experimental.pallas import tpu as pltpu

F32 = jnp.float32
BF16 = jnp.bfloat16

CONV_W = 31
N_KV = 4
GROUP = 4
HEAD_DIM = 64
CMP_BLOCK = 32
CMP_STRIDE = 16
CMP_HID = 128
SEL_BLOCK = 64
N_SELECT = 16
WINDOW = 512
DEPTH = 2
ALPHA = (2 * DEPTH) ** 0.25
LN_EPS = 1e-5
NEG = -1e30

VMEM_LIMIT = 56 * 1024 * 1024
HALO = 32


def _cparams(sem):
    return pltpu.CompilerParams(dimension_semantics=sem, vmem_limit_bytes=VMEM_LIMIT)


def _sigmoid(x):
    return 1.0 / (1.0 + jnp.exp(-x))


def _layer_norm(x, g, b):
    mu = jnp.mean(x, axis=-1, keepdims=True)
    xc = x - mu
    var = jnp.mean(xc * xc, axis=-1, keepdims=True)
    return xc * lax.rsqrt(var + LN_EPS) * g + b


def _dot(a, b):
    return jnp.dot(a, b, preferred_element_type=F32)


def _conv_layer_kernel(x_ref, st_ref, win_ref, bin_ref, dww_ref, dwb_ref, clg_ref, clb_ref, wout_ref,
                       lng_ref, lnb_ref, y_ref, ns_ref, ubuf, zbuf, cbuf, *, ns, ts, d_in, carry):
    t = pl.program_id(1)
    rows = ns * ts

    if carry:
        @pl.when(t == 0)
        def _():
            ubuf[:, 0:HALO, :] = jnp.zeros((ns, HALO, d_in), F32)
    else:
        ubuf[:, 0:HALO, :] = st_ref[...]

    x = x_ref[...].reshape(rows, x_ref.shape[-1])
    xb = x.astype(BF16)
    a = _dot(xb, win_ref[:, 0:d_in]) + bin_ref[:, 0:d_in]
    gl = _dot(xb, win_ref[:, d_in:2 * d_in]) + bin_ref[:, d_in:2 * d_in]
    ubuf[:, HALO:HALO + ts, :] = (a * _sigmoid(gl)).reshape(ns, ts, d_in)
    z = _dot(xb, win_ref[:, 2 * d_in:3 * d_in]) + bin_ref[:, 2 * d_in:3 * d_in]
    zbuf[...] = z * _sigmoid(z)

    rb = min(ts, 64)
    cb = 128
    n_rb = ts // rb
    n_cb = d_in // cb
    off0 = HALO - (CONV_W - 1)

    def chunk(i, c):
        s = i // (n_rb * n_cb)
        rem = i % (n_rb * n_cb)
        r0 = pl.multiple_of((rem // n_cb) * rb, rb)
        c0 = pl.multiple_of((rem % n_cb) * cb, cb)
        win = ubuf[s, pl.ds(r0, rb + HALO), pl.ds(c0, cb)]
        acc = jnp.zeros((rb, cb), F32) + dwb_ref[:, pl.ds(c0, cb)]
        for r in range(8):
            wr = win if r == 0 else win[r:r + rb + HALO - 8]
            for k in range(CONV_W):
                o = off0 + k
                if o % 8 == r:
                    acc = acc + dww_ref[k:k + 1, pl.ds(c0, cb)] * wr[o - r:o - r + rb]
        cbuf[s, pl.ds(r0, rb), pl.ds(c0, cb)] = acc
        return c

    lax.fori_loop(0, ns * n_rb * n_cb, chunk, 0)

    ns_ref[...] = ubuf[:, ts:ts + HALO, :]
    if carry:
        ubuf[:, 0:HALO, :] = ubuf[:, ts:ts + HALO, :]

    yc = _layer_norm(cbuf[...].reshape(rows, d_in), clg_ref[...], clb_ref[...])
    gated = (yc * _sigmoid(yc)) * zbuf[...]
    out = _dot(gated.astype(BF16), wout_ref[...])
    y = _layer_norm(ALPHA * x + out, lng_ref[...], lnb_ref[...])
    y_ref[...] = y.reshape(y_ref.shape)


def _conv_layer(x, state, w_in, b_in, dw_w, dw_b, cl_g, cl_b, w_out, ln_g, ln_b, *, ts, carry):
    S, T, D = x.shape
    d_in = w_out.shape[0]
    ns = 1 if carry else S
    grid = (S // ns, T // ts)
    const = lambda s, t: (0, 0)
    kernel = functools.partial(_conv_layer_kernel, ns=ns, ts=ts, d_in=d_in, carry=carry)
    y, new_state = pl.pallas_call(
        kernel,
        grid=grid,
        in_specs=[
            pl.BlockSpec((ns, ts, D), lambda s, t: (s, t, 0)),
            pl.BlockSpec((ns, HALO, d_in), lambda s, t: (s, 0, 0)),
            pl.BlockSpec(w_in.shape, const),
            pl.BlockSpec(b_in.shape, const),
            pl.BlockSpec(dw_w.shape, const),
            pl.BlockSpec(dw_b.shape, const),
            pl.BlockSpec(cl_g.shape, const),
            pl.BlockSpec(cl_b.shape, const),
            pl.BlockSpec(w_out.shape, const),
            pl.BlockSpec(ln_g.shape, const),
            pl.BlockSpec(ln_b.shape, const),
        ],
        out_specs=[
            pl.BlockSpec((ns, ts, D), lambda s, t: (s, t, 0)),
            pl.BlockSpec((ns, HALO, d_in), lambda s, t: (s, 0, 0)),
        ],
        out_shape=[
            jax.ShapeDtypeStruct((S, T, D), F32),
            jax.ShapeDtypeStruct((S, HALO, d_in), F32),
        ],
        scratch_shapes=[
            pltpu.VMEM((ns, HALO + ts, d_in), F32),
            pltpu.VMEM((ns * ts, d_in), F32),
            pltpu.VMEM((ns, ts, d_in), F32),
        ],
        compiler_params=_cparams(("arbitrary", "arbitrary")),
        name="conv_layer_carry" if carry else "conv_layer_state",
    )(x, state, w_in, b_in, dw_w, dw_b, cl_g, cl_b, w_out, ln_g, ln_b)
    return y, new_state


def _nsa_inproj_kernel(x_ref, w_ref, q_ref, kv_ref, sz_ref, g_ref, *, d_q, d_kv):
    xb = x_ref[...].astype(BF16)
    q_ref[...] = _dot(xb, w_ref[:, 0:d_q]).astype(BF16)
    kv_ref[...] = _dot(xb, w_ref[:, d_q:d_q + d_kv])
    z = _dot(xb, w_ref[:, d_q + d_kv:2 * d_q + d_kv])
    sz_ref[...] = z * _sigmoid(z)
    g_ref[...] = _sigmoid(_dot(xb, w_ref[:, 2 * d_q + d_kv:]))


def _nsa_inproj(x, w, *, d_q, d_kv, tm):
    N, D = x.shape
    d_g = w.shape[1] - 2 * d_q - d_kv
    row = lambda i: (i, 0)
    return pl.pallas_call(
        functools.partial(_nsa_inproj_kernel, d_q=d_q, d_kv=d_kv),
        grid=(N // tm,),
        in_specs=[pl.BlockSpec((tm, D), row), pl.BlockSpec(w.shape, lambda i: (0, 0))],
        out_specs=[pl.BlockSpec((tm, d_q), row), pl.BlockSpec((tm, d_kv), row),
                   pl.BlockSpec((tm, d_q), row), pl.BlockSpec((tm, d_g), row)],
        out_shape=[jax.ShapeDtypeStruct((N, d_q), BF16), jax.ShapeDtypeStruct((N, d_kv), F32),
                   jax.ShapeDtypeStruct((N, d_q), F32), jax.ShapeDtypeStruct((N, d_g), F32)],
        compiler_params=_cparams(("arbitrary",)),
        name="nsa_inproj",
    )(x, w)


def _compress_kernel(x_ref, pe_ref, w1_ref, w1s_ref, b1_ref, w2_ref, o_ref):
    x = x_ref[0, 0]
    nchunk = x.shape[0]
    parts = _dot(x, w1s_ref[0])
    base = _dot(pe_ref[0], w1_ref[0])[0:1, :] + b1_ref[0]
    second = pltpu.roll(parts[:, CMP_HID:], nchunk - 1, 0)
    row = lax.broadcasted_iota(jnp.int32, (nchunk, CMP_HID), 0)
    second = jnp.where(row < nchunk - 1, second, 0.0)
    h = base + parts[:, :CMP_HID] + second
    act = h * _sigmoid(h)
    o_ref[0, 0] = _dot(act.astype(BF16), w2_ref[0]).astype(BF16)


def _compress(x, pe, w1, w1s, b1, w2):
    n_st, BG, nchunk, K = x.shape
    st = lambda s, i: (s, 0, 0)
    return pl.pallas_call(
        _compress_kernel,
        grid=(n_st, BG),
        in_specs=[
            pl.BlockSpec((1, 1, nchunk, K), lambda s, i: (s, i, 0, 0)),
            pl.BlockSpec((1,) + pe.shape[1:], st),
            pl.BlockSpec((1,) + w1.shape[1:], st),
            pl.BlockSpec((1,) + w1s.shape[1:], st),
            pl.BlockSpec((1,) + b1.shape[1:], st),
            pl.BlockSpec((1,) + w2.shape[1:], st),
        ],
        out_specs=pl.BlockSpec((1, 1, nchunk, HEAD_DIM), lambda s, i: (s, i, 0, 0)),
        out_shape=jax.ShapeDtypeStruct((n_st, BG, nchunk, HEAD_DIM), BF16),
        compiler_params=_cparams(("arbitrary", "arbitrary")),
        name="compress",
    )(x, pe, w1, w1s, b1, w2)


def _softmax_cols(s, valid):
    s = jnp.where(valid, s, -jnp.inf)
    m = jnp.max(s, axis=0, keepdims=True)
    m = jnp.where(m > -jnp.inf, m, 0.0)
    e = jnp.where(valid, jnp.exp(s - m), 0.0)
    return e / jnp.maximum(jnp.sum(e, axis=0, keepdims=True), 1e-30)


def _select_blocks(selimp, cur, n_select):
    ns = selimp.shape[0]
    j = lax.broadcasted_iota(jnp.int32, selimp.shape, 0)
    causal = j <= cur
    forced = (j == 0) | (j == cur) | (j == cur - 1)
    score = jnp.where(forced, jnp.inf, jnp.where(causal, selimp, -jnp.inf))
    taken = jnp.zeros(selimp.shape, jnp.int32)
    for _ in range(n_select):
        m = jnp.max(score, axis=0, keepdims=True)
        first = jnp.min(jnp.where(score == m, j, ns), axis=0, keepdims=True)
        hit = j == first
        taken = jnp.where(hit, 1, taken)
        score = jnp.where(hit, -jnp.inf, score)
    return (taken > 0) & causal


def _cmp_to_sel_matrix(n_sel, n_cmp):
    r = SEL_BLOCK // CMP_STRIDE
    lo = -(CMP_BLOCK // CMP_STRIDE - 1)
    m = np.zeros((n_sel, n_cmp), np.float32)
    for o in range(lo, r):
        start = o * CMP_STRIDE
        ov = max(0, min(SEL_BLOCK, start + CMP_BLOCK) - max(0, start))
        for jb in range(n_sel):
            n = r * jb + o
            if ov > 0 and 0 <= n < n_cmp:
                m[jb, n] += ov / CMP_BLOCK
    return m


def _prompt_attn_kernel(q_ref, g_ref, kc_ref, vct_ref, ks_ref, vst_ref, kw_ref, vwt_ref, mt_ref, o_ref,
                        bias_ref, *, tq, kt_size):
    qi = pl.program_id(2)
    qs = qi * tq
    R = GROUP * tq
    n_cmp = kc_ref.shape[2]
    n_sel = mt_ref.shape[0]

    qT = jnp.concatenate([q_ref[0, 0, r] for r in range(GROUP)], axis=1)
    t1 = qs + lax.broadcasted_iota(jnp.int32, (1, tq), 1)
    t_lane = jnp.concatenate([t1] * GROUP, axis=1)

    sc = _dot(kc_ref[0, 0], qT)
    n_idx = lax.broadcasted_iota(jnp.int32, (n_cmp, R), 0)
    p_cmp = _softmax_cols(sc, n_idx * CMP_STRIDE + (CMP_BLOCK - 1) <= t_lane)
    o_cmp = _dot(vct_ref[0, 0], p_cmp.astype(BF16))

    imp = p_cmp[:, 0:tq]
    for r in range(1, GROUP):
        imp = imp + p_cmp[:, r * tq:(r + 1) * tq]
    selimp = jnp.dot(mt_ref[...], imp, precision=lax.Precision.HIGHEST, preferred_element_type=F32)
    sel = _select_blocks(selimp, t1 // SEL_BLOCK, N_SELECT)
    bias = jnp.where(sel, 0.0, NEG)
    bias_ref[...] = jnp.concatenate([bias] * GROUP, axis=1)

    blk_per_tile = kt_size // SEL_BLOCK

    def sel_tile(kt, carry):
        m, l, acc = carry
        k0 = pl.multiple_of(kt * kt_size, kt_size)
        s = _dot(ks_ref[0, 0, pl.ds(k0, kt_size), :], qT)
        b_t = bias_ref[pl.ds(pl.multiple_of(kt * blk_per_tile, blk_per_tile), blk_per_tile), :]
        s = jnp.concatenate(
            [s[jb * SEL_BLOCK:(jb + 1) * SEL_BLOCK] + b_t[jb:jb + 1, :] for jb in range(blk_per_tile)], axis=0)
        kidx = k0 + lax.broadcasted_iota(jnp.int32, (kt_size, R), 0)
        s = jnp.where(kidx <= t_lane, s, NEG)
        m_new = jnp.maximum(m, jnp.max(s, axis=0, keepdims=True))
        alpha = jnp.exp(m - m_new)
        p = jnp.exp(s - m_new)
        l = alpha * l + jnp.sum(p, axis=0, keepdims=True)
        acc = alpha * acc + _dot(vst_ref[0, 0, :, pl.ds(k0, kt_size)], p.astype(BF16))
        return m_new, l, acc

    n_kt = (qs + tq + kt_size - 1) // kt_size
    init = (jnp.full((1, R), NEG, F32), jnp.zeros((1, R), F32), jnp.zeros((HEAD_DIM, R), F32))
    _, l_sel, acc_sel = lax.fori_loop(0, n_kt, sel_tile, init)
    o_sel = acc_sel / jnp.maximum(l_sel, 1e-30)

    wk = WINDOW + tq
    start = pl.multiple_of(jnp.maximum(qs - WINDOW, 0), 128)
    sw = _dot(kw_ref[0, 0, pl.ds(start, wk), :], qT)
    dpos = t_lane - (start + lax.broadcasted_iota(jnp.int32, (wk, R), 0))
    p_win = _softmax_cols(sw, (dpos >= 0) & (dpos < WINDOW))
    o_win = _dot(vwt_ref[0, 0, :, pl.ds(start, wk)], p_win.astype(BF16))

    def gate(branch):
        return jnp.concatenate([g_ref[0, 0, branch * GROUP + r:branch * GROUP + r + 1, :] for r in range(GROUP)], axis=1)

    o = gate(0) * o_cmp + gate(1) * o_sel + gate(2) * o_win
    for r in range(GROUP):
        o_ref[0, 0, r] = o[:, r * tq:(r + 1) * tq]


def _prompt_attn(qT, gT, kc, vcT, ks, vsT, kw, vwT, mT, *, tq, kt_size):
    B, G, _, Dh, T = qT.shape
    NC = kc.shape[2]
    bg = lambda b, g, i: (b, g, 0, 0)
    return pl.pallas_call(
        functools.partial(_prompt_attn_kernel, tq=tq, kt_size=kt_size),
        grid=(B, G, T // tq),
        in_specs=[
            pl.BlockSpec((1, 1, GROUP, Dh, tq), lambda b, g, i: (b, g, 0, 0, i)),
            pl.BlockSpec((1, 1, gT.shape[2], tq), lambda b, g, i: (b, g, 0, i)),
            pl.BlockSpec((1, 1, NC, Dh), bg),
            pl.BlockSpec((1, 1, Dh, NC), bg),
            pl.BlockSpec((1, 1, T, Dh), bg),
            pl.BlockSpec((1, 1, Dh, T), bg),
            pl.BlockSpec((1, 1, T, Dh), bg),
            pl.BlockSpec((1, 1, Dh, T), bg),
            pl.BlockSpec(mT.shape, lambda b, g, i: (0, 0)),
        ],
        out_specs=pl.BlockSpec((1, 1, GROUP, Dh, tq), lambda b, g, i: (b, g, 0, 0, i)),
        out_shape=jax.ShapeDtypeStruct((B, G, GROUP, Dh, T), F32),
        scratch_shapes=[pltpu.VMEM((mT.shape[0], GROUP * tq), F32)],
        compiler_params=_cparams(("arbitrary", "arbitrary", "arbitrary")),
        name="prompt_attn",
    )(qT, gT, kc, vcT, ks, vsT, kw, vwT, mT)


def _dot_tn(v, p):
    return lax.dot_general(v, p, (((0,), (0,)), ((), ())), preferred_element_type=F32)


def _sample_attn_kernel(q_ref, g_ref, kc_ref, vc_ref, ks_ref, vs_ref, kn_ref, vn_ref, kw_ref, vw_ref, mt_ref,
                        rr_ref, o_ref, bias_ref, *, t_new, past, kt_size):
    R = GROUP * t_new
    n_cmp = kc_ref.shape[2]
    qT = q_ref[0, 0]
    i_lane = lax.broadcasted_iota(jnp.int32, (1, R), 1) % t_new
    t_lane = past + i_lane

    sc = _dot(kc_ref[0, 0], qT)
    n_idx = lax.broadcasted_iota(jnp.int32, (n_cmp, R), 0)
    p_cmp = _softmax_cols(sc, n_idx * CMP_STRIDE + (CMP_BLOCK - 1) <= t_lane)
    o_cmp = _dot_tn(vc_ref[0, 0], p_cmp.astype(BF16))

    hi = lax.Precision.HIGHEST
    imp = jnp.dot(p_cmp, rr_ref[...], precision=hi, preferred_element_type=F32)
    selimp = jnp.dot(mt_ref[...], imp, precision=hi, preferred_element_type=F32)
    sel = _select_blocks(selimp, t_lane // SEL_BLOCK, N_SELECT)
    bias_ref[...] = jnp.where(sel, 0.0, NEG)

    blk_per_tile = kt_size // SEL_BLOCK

    def update(carry, s, v):
        m, l, acc = carry
        m_new = jnp.maximum(m, jnp.max(s, axis=0, keepdims=True))
        alpha = jnp.exp(m - m_new)
        p = jnp.exp(s - m_new)
        l = alpha * l + jnp.sum(p, axis=0, keepdims=True)
        acc = alpha * acc + _dot_tn(v, p.astype(BF16))
        return m_new, l, acc

    def sel_tile(kt, carry):
        k0 = pl.multiple_of(kt * kt_size, kt_size)
        s = _dot(ks_ref[0, 0, pl.ds(k0, kt_size), :], qT)
        b_t = bias_ref[pl.ds(pl.multiple_of(kt * blk_per_tile, blk_per_tile), blk_per_tile), :]
        s = jnp.concatenate(
            [s[jb * SEL_BLOCK:(jb + 1) * SEL_BLOCK] + b_t[jb:jb + 1, :] for jb in range(blk_per_tile)], axis=0)
        return update(carry, s, vs_ref[0, 0, pl.ds(k0, kt_size), :])

    init = (jnp.full((1, R), NEG, F32), jnp.zeros((1, R), F32), jnp.zeros((HEAD_DIM, R), F32))
    carry = lax.fori_loop(0, past // kt_size, sel_tile, init)
    new_blk = past // SEL_BLOCK
    s_new = _dot(kn_ref[0, 0], qT) + bias_ref[new_blk:new_blk + 1, :]
    ip = lax.broadcasted_iota(jnp.int32, (t_new, R), 0)
    s_new = jnp.where(ip <= i_lane, s_new, NEG)
    _, l_sel, acc_sel = update(carry, s_new, vn_ref[0, 0])
    o_sel = acc_sel / jnp.maximum(l_sel, 1e-30)

    n_w = kw_ref.shape[2]
    wbuf = n_w - t_new
    sw = _dot(kw_ref[0, 0], qT)
    jw = lax.broadcasted_iota(jnp.int32, (n_w, R), 0)
    dpos = wbuf + i_lane - jw
    p_win = _softmax_cols(sw, (dpos >= 0) & (dpos < WINDOW) & (past - wbuf + jw >= 0))
    o_win = _dot_tn(vw_ref[0, 0], p_win.astype(BF16))

    o_ref[0, 0] = g_ref[0, 0, 0:1, :] * o_cmp + g_ref[0, 0, 1:2, :] * o_sel + g_ref[0, 0, 2:3, :] * o_win


def _sample_attn(qT, gT, kc, vc, ks, vs, kn, vn, kw, vw, mT, rr, *, t_new, kt_size):
    DB, G, Dh, R = qT.shape
    past = ks.shape[2]
    bg = lambda b, g: (b, g, 0, 0)
    full = lambda a: pl.BlockSpec((1, 1) + a.shape[2:], bg)
    const = lambda a: pl.BlockSpec(a.shape, lambda b, g: (0, 0))
    return pl.pallas_call(
        functools.partial(_sample_attn_kernel, t_new=t_new, past=past, kt_size=kt_size),
        grid=(DB, G),
        in_specs=[full(qT), full(gT), full(kc), full(vc), full(ks), full(vs), full(kn), full(vn), full(kw), full(vw),
                  const(mT), const(rr)],
        out_specs=pl.BlockSpec((1, 1, Dh, R), bg),
        out_shape=jax.ShapeDtypeStruct((DB, G, Dh, R), F32),
        scratch_shapes=[pltpu.VMEM((mT.shape[0], R), F32)],
        compiler_params=_cparams(("arbitrary", "arbitrary")),
        name="sample_attn",
    )(qT, gT, kc, vc, ks, vs, kn, vn, kw, vw, mT, rr)


PAGES_PER_STEP = 8


def _gather_kernel(pt_ref, *refs, page, n_heads):
    del pt_ref
    pages, o_ref = refs[:-1], refs[-1]
    half = page // 2
    low = lax.broadcasted_iota(jnp.int32, (half, 128), 1) < HEAD_DIM
    for j, p_ref in enumerate(pages):
        for c in range(n_heads // 2):
            ev = p_ref[0, pl.ds(c, half, stride=16), :]
            od = p_ref[0, pl.ds(8 + c, half, stride=16), :]
            h0 = jnp.where(low, ev, pltpu.roll(od, HEAD_DIM, 1))
            h1 = jnp.where(low, pltpu.roll(ev, HEAD_DIM, 1), od)
            for h, val in ((2 * c, h0), (2 * c + 1, h1)):
                o_ref[h // N_KV, 0, h % N_KV, j * half:(j + 1) * half, :] = val.astype(BF16)


def _gather_pages(cache, page_table):
    n_pool, page, n_st, G, Dh = cache.shape
    DB, n_pages = page_table.shape
    n_heads = n_st * G
    assert n_heads * Dh == 8 * 128 and n_pages % PAGES_PER_STEP == 0
    view = cache.reshape(n_pool, page * 8, 128)
    half = page // 2

    def page_spec(j):
        return pl.BlockSpec((1, page * 8, 128), lambda b, p, pt: (pt[b, p * PAGES_PER_STEP + j], 0, 0))

    out = pl.pallas_call(
        functools.partial(_gather_kernel, page=page, n_heads=n_heads),
        grid_spec=pltpu.PrefetchScalarGridSpec(
            num_scalar_prefetch=1,
            grid=(DB, n_pages // PAGES_PER_STEP),
            in_specs=[page_spec(j) for j in range(PAGES_PER_STEP)],
            out_specs=pl.BlockSpec((n_st, 1, G, PAGES_PER_STEP * half, 128), lambda b, p, pt: (0, b, 0, p, 0)),
        ),
        out_shape=jax.ShapeDtypeStruct((n_st, DB, G, n_pages * half, 128), BF16),
        compiler_params=_cparams(("arbitrary", "arbitrary")),
        name="gather_pages",
    )(page_table, *([view] * PAGES_PER_STEP))
    return out.reshape(n_st, DB, G, n_pages * page, Dh)


def _nsa_outproj_kernel(o_ref, sz_ref, x_ref, w_ref, g_ref, b_ref, y_ref):
    gated = (o_ref[...] * sz_ref[...]).astype(BF16)
    y = _dot(gated, w_ref[...])
    y_ref[...] = _layer_norm(ALPHA * x_ref[...] + y, g_ref[...], b_ref[...])


def _nsa_outproj(o, sz, x, w, g, b, *, tm):
    N, D = x.shape
    row = lambda i: (i, 0)
    const = lambda i: (0, 0)
    return pl.pallas_call(
        _nsa_outproj_kernel,
        grid=(N // tm,),
        in_specs=[pl.BlockSpec((tm, o.shape[1]), row), pl.BlockSpec((tm, sz.shape[1]), row),
                  pl.BlockSpec((tm, D), row), pl.BlockSpec(w.shape, const),
                  pl.BlockSpec(g.shape, const), pl.BlockSpec(b.shape, const)],
        out_specs=pl.BlockSpec((tm, D), row),
        out_shape=jax.ShapeDtypeStruct((N, D), F32),
        compiler_params=_cparams(("arbitrary",)),
        name="nsa_outproj",
    )(o, sz, x, w, g, b)


def _prep_nsa_weights(w_in, pe, w1, b1, w2):
    D = w_in.shape[0]
    att_w = N_KV * GROUP * HEAD_DIM
    kv_w = 6 * N_KV * HEAD_DIM
    o1, o2 = att_w, att_w + kv_w
    o3 = o2 + 3 * N_KV * GROUP
    w_q = w_in[:, :o1] * (HEAD_DIM ** -0.5)
    w_g = w_in[:, o2:o3].reshape(D, N_KV, GROUP, 3).transpose(0, 3, 1, 2).reshape(D, 3 * N_KV * GROUP)
    w_g = jnp.pad(w_g, ((0, 0), (0, 128 - w_g.shape[1])))
    w_all = jnp.concatenate([w_q, w_in[:, o1:o2], w_in[:, o3:], w_g], axis=1).astype(BF16)
    r = CMP_BLOCK // CMP_STRIDE
    pe_rows = jnp.pad(pe.reshape(2, 1, CMP_BLOCK * HEAD_DIM), ((0, 0), (0, 7), (0, 0))).astype(BF16)
    w1_flat = w1.reshape(2, CMP_BLOCK * HEAD_DIM, CMP_HID).astype(BF16)
    w1r = w1.reshape(2, r, CMP_STRIDE * HEAD_DIM, CMP_HID)
    w1_split = jnp.concatenate([w1r[:, j] for j in range(r)], axis=-1).astype(BF16)
    return w_all, pe_rows, w1_flat, w1_split, b1.reshape(2, 1, CMP_HID), w2.astype(BF16)


def _nsa_prompt(x1, nsa_w, w_out, g, b):
    B, T, D = x1.shape
    w_all, pe_rows, w1_flat, w1_split, b1, w2 = nsa_w
    att_w = N_KV * GROUP * HEAD_DIM
    kv_w = 6 * N_KV * HEAD_DIM
    x_flat = x1.reshape(B * T, D)
    q, kv, sz, gates = _nsa_inproj(x_flat, w_all, d_q=att_w, d_kv=kv_w, tm=512)
    kv6 = kv.reshape(B, T, 6, N_KV, HEAD_DIM)
    rows = kv6[:, :, :4]
    win = kv6[:, T - min(WINDOW, T):, 4:]
    qT = q.reshape(B, T, N_KV, GROUP, HEAD_DIM).transpose(0, 2, 3, 4, 1)
    gT = gates[:, :3 * N_KV * GROUP].reshape(B, T, 3, N_KV, GROUP).transpose(0, 3, 2, 4, 1)
    gT = jnp.pad(gT.reshape(B, N_KV, 3 * GROUP, T), ((0, 0), (0, 0), (0, 16 - 3 * GROUP), (0, 0)))
    kvb = kv6.astype(BF16).transpose(2, 0, 3, 1, 4)
    nchunk = T // CMP_STRIDE
    x_cmp = kvb[0:2].reshape(2, B * N_KV, nchunk, CMP_STRIDE * HEAD_DIM)
    kcv = _compress(x_cmp, pe_rows, w1_flat, w1_split, b1, w2).reshape(2, B, N_KV, nchunk, HEAD_DIM)
    mT = jnp.asarray(_cmp_to_sel_matrix(T // SEL_BLOCK, nchunk))
    tr = lambda a: a.transpose(0, 1, 3, 2)
    oT = _prompt_attn(qT, gT, kcv[0], tr(kcv[1]), kvb[2], tr(kvb[3]), kvb[4], tr(kvb[5]), mT, tq=128, kt_size=512)
    o = oT.transpose(0, 4, 1, 2, 3).reshape(B * T, att_w)
    y = _nsa_outproj(o, sz, x_flat, w_out, g, b, tm=512)
    return y.reshape(B, T, D), rows, win


def _nsa_sample(x1, cache_kv_l, cache_win_l, page_table, nsa_w, w_out, g, b):
    DB, T, D = x1.shape
    w_all, pe_rows, w1_flat, w1_split, b1, w2 = nsa_w
    att_w = N_KV * GROUP * HEAD_DIM
    kv_w = 6 * N_KV * HEAD_DIM
    page = cache_kv_l.shape[1]
    past = page_table.shape[1] * page
    assert T <= CMP_STRIDE and past % SEL_BLOCK == 0 and past % CMP_STRIDE == 0
    x_flat = x1.reshape(DB * T, D)
    q, kv, sz, gates = _nsa_inproj(x_flat, w_all, d_q=att_w, d_kv=kv_w, tm=DB * T)
    kv6 = kv.reshape(DB, T, 6, N_KV, HEAD_DIM)
    rows = kv6[:, :, :4]
    wseq = jnp.concatenate([cache_win_l, kv6[:, :, 4:]], axis=1)
    R = GROUP * T
    qT = q.reshape(DB, T, N_KV, GROUP, HEAD_DIM).transpose(0, 2, 4, 3, 1).reshape(DB, N_KV, HEAD_DIM, R)
    gT = gates[:, :3 * N_KV * GROUP].reshape(DB, T, 3, N_KV, GROUP).transpose(0, 3, 2, 4, 1).reshape(DB, N_KV, 3, R)
    gT = jnp.pad(gT, ((0, 0), (0, 0), (0, 5), (0, 0)))
    to_bg = lambda a: a.astype(BF16).transpose(0, 2, 1, 3)
    past4 = _gather_pages(cache_kv_l, page_table)
    nchunk = past // CMP_STRIDE
    x_cmp = past4[0:2].reshape(2, DB * N_KV, nchunk, CMP_STRIDE * HEAD_DIM)
    kcv = _compress(x_cmp, pe_rows, w1_flat, w1_split, b1, w2).reshape(2, DB, N_KV, nchunk, HEAD_DIM)
    n_sel = past // SEL_BLOCK + 1
    n_sel_pad = -(-n_sel // 8) * 8
    mT = np.zeros((n_sel_pad, nchunk), np.float32)
    mT[:n_sel] = _cmp_to_sel_matrix(n_sel, nchunk)
    lane = np.arange(R)
    rr = (lane[:, None] % T == lane[None, :] % T).astype(np.float32)
    oT = _sample_attn(qT, gT, kcv[0], kcv[1], past4[2], past4[3], to_bg(kv6[:, :, 2]), to_bg(kv6[:, :, 3]),
                      to_bg(wseq[:, :, 0]), to_bg(wseq[:, :, 1]), jnp.asarray(mT), jnp.asarray(rr),
                      t_new=T, kt_size=512)
    o = oT.reshape(DB, N_KV, HEAD_DIM, GROUP, T).transpose(0, 4, 1, 3, 2).reshape(DB * T, att_w)
    y = _nsa_outproj(o, sz, x_flat, w_out, g, b, tm=DB * T)
    return y.reshape(DB, T, D), rows, wseq[:, T:]


def kernel(x_prompt, x_sample, state_conv, cache_kv, cache_win, page_table, conv_w_in, conv_b_in, conv_dw_w,
           conv_dw_b, conv_ln_g, conv_ln_b, conv_w_out, nsa_w_in, cmp_pe, cmp_w1, cmp_b1, cmp_w2, nsa_w_out,
           ln_g, ln_b):
    B, T, D = x_prompt.shape
    DB, TS, _ = x_sample.shape
    d_in = conv_w_out.shape[1]
    row = lambda v: v.reshape(1, -1)

    cw = (conv_w_in[0].astype(BF16), row(conv_b_in[0]), conv_dw_w[0], row(conv_dw_b[0]), row(conv_ln_g[0]),
          row(conv_ln_b[0]), conv_w_out[0].astype(BF16), row(ln_g[0]), row(ln_b[0]))
    pad = HALO - (CONV_W - 1)
    xp, sp = _conv_layer(x_prompt, jnp.zeros((B, HALO, d_in), F32), *cw, ts=256, carry=True)
    st = jnp.pad(state_conv[0], ((0, 0), (pad, 0), (0, 0)))
    xs, ss = _conv_layer(x_sample, st, *cw, ts=TS, carry=False)

    nsa_w = _prep_nsa_weights(nsa_w_in[0], cmp_pe[0], cmp_w1[0], cmp_b1[0], cmp_w2[0])
    w_out = nsa_w_out[0].astype(BF16)
    yp, rp, wp = _nsa_prompt(xp, nsa_w, w_out, row(ln_g[1]), row(ln_b[1]))
    ys, rs, ws = _nsa_sample(xs, cache_kv[0], cache_win[0], page_table, nsa_w, w_out, row(ln_g[1]), row(ln_b[1]))
    return (yp, ys, sp[None, :, pad:], ss[None, :, pad:], rp[None], rs[None], wp[None], ws[None])
```

```python
import functools

import jax
import jax.numpy as jnp
import numpy as np
from jax import lax
from jax.experimental import pallas as pl
from jax.experimental.pallas import tpu as pltpu

F32 = jnp.float32
BF16 = jnp.bfloat16

CONV_W = 31
N_KV = 4
GROUP = 4
HEAD_DIM = 64
CMP_BLOCK = 32
CMP_STRIDE = 16
CMP_HID = 128
SEL_BLOCK = 64
N_SELECT = 16
WINDOW = 512
DEPTH = 2
ALPHA = (2 * DEPTH) ** 0.25
LN_EPS = 1e-5
NEG = -1e30

VMEM_LIMIT = 56 * 1024 * 1024
HALO = 32


def _cparams(sem):
    return pltpu.CompilerParams(dimension_semantics=sem, vmem_limit_bytes=VMEM_LIMIT)


def _sigmoid(x):
    return 1.0 / (1.0 + jnp.exp(-x))


def _layer_norm(x, g, b):
    mu = jnp.mean(x, axis=-1, keepdims=True)
    xc = x - mu
    var = jnp.mean(xc * xc, axis=-1, keepdims=True)
    return xc * lax.rsqrt(var + LN_EPS) * g + b


def _dot(a, b):
    return jnp.dot(a, b, preferred_element_type=F32)


def _conv_layer_kernel(x_ref, st_ref, win_ref, bin_ref, dww_ref, dwb_ref, clg_ref, clb_ref, wout_ref,
                       lng_ref, lnb_ref, y_ref, ns_ref, ubuf, zbuf, cbuf, *, ns, ts, d_in, carry):
    t = pl.program_id(1)
    rows = ns * ts

    if carry:
        @pl.when(t == 0)
        def _():
            ubuf[:, 0:HALO, :] = jnp.zeros((ns, HALO, d_in), F32)
    else:
        ubuf[:, 0:HALO, :] = st_ref[...]

    x = x_ref[...].reshape(rows, x_ref.shape[-1])
    xb = x.astype(BF16)
    a = _dot(xb, win_ref[:, 0:d_in]) + bin_ref[:, 0:d_in]
    gl = _dot(xb, win_ref[:, d_in:2 * d_in]) + bin_ref[:, d_in:2 * d_in]
    ubuf[:, HALO:HALO + ts, :] = (a * _sigmoid(gl)).reshape(ns, ts, d_in)
    z = _dot(xb, win_ref[:, 2 * d_in:3 * d_in]) + bin_ref[:, 2 * d_in:3 * d_in]
    zbuf[...] = z * _sigmoid(z)

    rb = min(ts, 64)
    cb = 128
    n_rb = ts // rb
    n_cb = d_in // cb
    off0 = HALO - (CONV_W - 1)

    def chunk(i, c):
        s = i // (n_rb * n_cb)
        rem = i % (n_rb * n_cb)
        r0 = pl.multiple_of((rem // n_cb) * rb, rb)
        c0 = pl.multiple_of((rem % n_cb) * cb, cb)
        win = ubuf[s, pl.ds(r0, rb + HALO), pl.ds(c0, cb)]
        acc = jnp.zeros((rb, cb), F32) + dwb_ref[:, pl.ds(c0, cb)]
        for r in range(8):
            wr = win if r == 0 else win[r:r + rb + HALO - 8]
            for k in range(CONV_W):
                o = off0 + k
                if o % 8 == r:
                    acc = acc + dww_ref[k:k + 1, pl.ds(c0, cb)] * wr[o - r:o - r + rb]
        cbuf[s, pl.ds(r0, rb), pl.ds(c0, cb)] = acc
        return c

    lax.fori_loop(0, ns * n_rb * n_cb, chunk, 0)

    ns_ref[...] = ubuf[:, ts:ts + HALO, :]
    if carry:
        ubuf[:, 0:HALO, :] = ubuf[:, ts:ts + HALO, :]

    yc = _layer_norm(cbuf[...].reshape(rows, d_in), clg_ref[...], clb_ref[...])
    gated = (yc * _sigmoid(yc)) * zbuf[...]
    out = _dot(gated.astype(BF16), wout_ref[...])
    y = _layer_norm(ALPHA * x + out, lng_ref[...], lnb_ref[...])
    y_ref[...] = y.reshape(y_ref.shape)


def _conv_layer(x, state, w_in, b_in, dw_w, dw_b, cl_g, cl_b, w_out, ln_g, ln_b, *, ts, carry):
    S, T, D = x.shape
    d_in = w_out.shape[0]
    ns = 1 if carry else S
    grid = (S // ns, T // ts)
    const = lambda s, t: (0, 0)
    kernel = functools.partial(_conv_layer_kernel, ns=ns, ts=ts, d_in=d_in, carry=carry)
    y, new_state = pl.pallas_call(
        kernel,
        grid=grid,
        in_specs=[
            pl.BlockSpec((ns, ts, D), lambda s, t: (s, t, 0)),
            pl.BlockSpec((ns, HALO, d_in), lambda s, t: (s, 0, 0)),
            pl.BlockSpec(w_in.shape, const),
            pl.BlockSpec(b_in.shape, const),
            pl.BlockSpec(dw_w.shape, const),
            pl.BlockSpec(dw_b.shape, const),
            pl.BlockSpec(cl_g.shape, const),
            pl.BlockSpec(cl_b.shape, const),
            pl.BlockSpec(w_out.shape, const),
            pl.BlockSpec(ln_g.shape, const),
            pl.BlockSpec(ln_b.shape, const),
        ],
        out_specs=[
            pl.BlockSpec((ns, ts, D), lambda s, t: (s, t, 0)),
            pl.BlockSpec((ns, HALO, d_in), lambda s, t: (s, 0, 0)),
        ],
        out_shape=[
            jax.ShapeDtypeStruct((S, T, D), F32),
            jax.ShapeDtypeStruct((S, HALO, d_in), F32),
        ],
        scratch_shapes=[
            pltpu.VMEM((ns, HALO + ts, d_in), F32),
            pltpu.VMEM((ns * ts, d_in), F32),
            pltpu.VMEM((ns, ts, d_in), F32),
        ],
        compiler_params=_cparams(("arbitrary", "arbitrary")),
        name="conv_layer_carry" if carry else "conv_layer_state",
    )(x, state, w_in, b_in, dw_w, dw_b, cl_g, cl_b, w_out, ln_g, ln_b)
    return y, new_state


def _nsa_inproj_kernel(x_ref, w_ref, q_ref, kv_ref, sz_ref, g_ref, *, d_q, d_kv):
    xb = x_ref[...].astype(BF16)
    q_ref[...] = _dot(xb, w_ref[:, 0:d_q]).astype(BF16)
    kv_ref[...] = _dot(xb, w_ref[:, d_q:d_q + d_kv])
    z = _dot(xb, w_ref[:, d_q + d_kv:2 * d_q + d_kv])
    sz_ref[...] = z * _sigmoid(z)
    g_ref[...] = _sigmoid(_dot(xb, w_ref[:, 2 * d_q + d_kv:]))


def _nsa_inproj(x, w, *, d_q, d_kv, tm):
    N, D = x.shape
    d_g = w.shape[1] - 2 * d_q - d_kv
    row = lambda i: (i, 0)
    return pl.pallas_call(
        functools.partial(_nsa_inproj_kernel, d_q=d_q, d_kv=d_kv),
        grid=(N // tm,),
        in_specs=[pl.BlockSpec((tm, D), row), pl.BlockSpec(w.shape, lambda i: (0, 0))],
        out_specs=[pl.BlockSpec((tm, d_q), row), pl.BlockSpec((tm, d_kv), row),
                   pl.BlockSpec((tm, d_q), row), pl.BlockSpec((tm, d_g), row)],
        out_shape=[jax.ShapeDtypeStruct((N, d_q), BF16), jax.ShapeDtypeStruct((N, d_kv), F32),
                   jax.ShapeDtypeStruct((N, d_q), F32), jax.ShapeDtypeStruct((N, d_g), F32)],
        compiler_params=_cparams(("arbitrary",)),
        name="nsa_inproj",
    )(x, w)


def _compress_finish(parts, base, w2):
    nchunk = parts.shape[0]
    second = pltpu.roll(parts[:, CMP_HID:], nchunk - 1, 0)
    row = lax.broadcasted_iota(jnp.int32, (nchunk, CMP_HID), 0)
    second = jnp.where(row < nchunk - 1, second, 0.0)
    h = base + parts[:, :CMP_HID] + second
    act = h * _sigmoid(h)
    return _dot(act.astype(BF16), w2).astype(BF16)


def _compress_kernel(x_ref, pe_ref, w1_ref, w1s_ref, b1_ref, w2_ref, o_ref):
    parts = _dot(x_ref[0, 0], w1s_ref[0])
    base = _dot(pe_ref[0], w1_ref[0])[0:1, :] + b1_ref[0]
    o_ref[0, 0] = _compress_finish(parts, base, w2_ref[0])


def _compress(x, pe, w1, w1s, b1, w2):
    n_st, BG, nchunk, K = x.shape
    st = lambda s, i: (s, 0, 0)
    return pl.pallas_call(
        _compress_kernel,
        grid=(n_st, BG),
        in_specs=[
            pl.BlockSpec((1, 1, nchunk, K), lambda s, i: (s, i, 0, 0)),
            pl.BlockSpec((1,) + pe.shape[1:], st),
            pl.BlockSpec((1,) + w1.shape[1:], st),
            pl.BlockSpec((1,) + w1s.shape[1:], st),
            pl.BlockSpec((1,) + b1.shape[1:], st),
            pl.BlockSpec((1,) + w2.shape[1:], st),
        ],
        out_specs=pl.BlockSpec((1, 1, nchunk, HEAD_DIM), lambda s, i: (s, i, 0, 0)),
        out_shape=jax.ShapeDtypeStruct((n_st, BG, nchunk, HEAD_DIM), BF16),
        compiler_params=_cparams(("arbitrary", "arbitrary")),
        name="compress",
    )(x, pe, w1, w1s, b1, w2)


def _softmax_cols(s, valid):
    s = jnp.where(valid, s, -jnp.inf)
    m = jnp.max(s, axis=0, keepdims=True)
    m = jnp.where(m > -jnp.inf, m, 0.0)
    e = jnp.where(valid, jnp.exp(s - m), 0.0)
    return e / jnp.maximum(jnp.sum(e, axis=0, keepdims=True), 1e-30)


def _select_blocks(selimp, cur, n_select):
    ns = selimp.shape[0]
    j = lax.broadcasted_iota(jnp.int32, selimp.shape, 0)
    causal = j <= cur
    forced = (j == 0) | (j == cur) | (j == cur - 1)
    score = jnp.where(forced, jnp.inf, jnp.where(causal, selimp, -jnp.inf))
    taken = jnp.zeros(selimp.shape, jnp.int32)
    for _ in range(n_select):
        m = jnp.max(score, axis=0, keepdims=True)
        first = jnp.min(jnp.where(score == m, j, ns), axis=0, keepdims=True)
        hit = j == first
        taken = jnp.where(hit, 1, taken)
        score = jnp.where(hit, -jnp.inf, score)
    return (taken > 0) & causal


def _cmp_to_sel_matrix(n_sel, n_cmp):
    r = SEL_BLOCK // CMP_STRIDE
    lo = -(CMP_BLOCK // CMP_STRIDE - 1)
    m = np.zeros((n_sel, n_cmp), np.float32)
    for o in range(lo, r):
        start = o * CMP_STRIDE
        ov = max(0, min(SEL_BLOCK, start + CMP_BLOCK) - max(0, start))
        for jb in range(n_sel):
            n = r * jb + o
            if ov > 0 and 0 <= n < n_cmp:
                m[jb, n] += ov / CMP_BLOCK
    return m


def _prompt_attn_kernel(q_ref, g_ref, kc_ref, vct_ref, ks_ref, vst_ref, kw_ref, vwt_ref, mt_ref, o_ref,
                        bias_ref, *, tq, kt_size):
    qi = pl.program_id(2)
    qs = qi * tq
    R = GROUP * tq
    n_cmp = kc_ref.shape[2]
    n_sel = mt_ref.shape[0]

    qT = jnp.concatenate([q_ref[0, 0, r] for r in range(GROUP)], axis=1)
    t1 = qs + lax.broadcasted_iota(jnp.int32, (1, tq), 1)
    t_lane = jnp.concatenate([t1] * GROUP, axis=1)

    sc = _dot(kc_ref[0, 0], qT)
    n_idx = lax.broadcasted_iota(jnp.int32, (n_cmp, R), 0)
    p_cmp = _softmax_cols(sc, n_idx * CMP_STRIDE + (CMP_BLOCK - 1) <= t_lane)
    o_cmp = _dot(vct_ref[0, 0], p_cmp.astype(BF16))

    imp = p_cmp[:, 0:tq]
    for r in range(1, GROUP):
        imp = imp + p_cmp[:, r * tq:(r + 1) * tq]
    selimp = jnp.dot(mt_ref[...], imp, precision=lax.Precision.HIGHEST, preferred_element_type=F32)
    sel = _select_blocks(selimp, t1 // SEL_BLOCK, N_SELECT)
    bias = jnp.where(sel, 0.0, NEG)
    bias_ref[...] = jnp.concatenate([bias] * GROUP, axis=1)

    blk_per_tile = kt_size // SEL_BLOCK

    def sel_tile(kt, carry):
        m, l, acc = carry
        k0 = pl.multiple_of(kt * kt_size, kt_size)
        s = _dot(ks_ref[0, 0, pl.ds(k0, kt_size), :], qT)
        b_t = bias_ref[pl.ds(pl.multiple_of(kt * blk_per_tile, blk_per_tile), blk_per_tile), :]
        s = jnp.concatenate(
            [s[jb * SEL_BLOCK:(jb + 1) * SEL_BLOCK] + b_t[jb:jb + 1, :] for jb in range(blk_per_tile)], axis=0)
        kidx = k0 + lax.broadcasted_iota(jnp.int32, (kt_size, R), 0)
        s = jnp.where(kidx <= t_lane, s, NEG)
        m_new = jnp.maximum(m, jnp.max(s, axis=0, keepdims=True))
        alpha = jnp.exp(m - m_new)
        p = jnp.exp(s - m_new)
        l = alpha * l + jnp.sum(p, axis=0, keepdims=True)
        acc = alpha * acc + _dot(vst_ref[0, 0, :, pl.ds(k0, kt_size)], p.astype(BF16))
        return m_new, l, acc

    n_kt = (qs + tq + kt_size - 1) // kt_size
    init = (jnp.full((1, R), NEG, F32), jnp.zeros((1, R), F32), jnp.zeros((HEAD_DIM, R), F32))
    _, l_sel, acc_sel = lax.fori_loop(0, n_kt, sel_tile, init)
    o_sel = acc_sel / jnp.maximum(l_sel, 1e-30)

    wk = WINDOW + tq
    start = pl.multiple_of(jnp.maximum(qs - WINDOW, 0), 128)
    sw = _dot(kw_ref[0, 0, pl.ds(start, wk), :], qT)
    dpos = t_lane - (start + lax.broadcasted_iota(jnp.int32, (wk, R), 0))
    p_win = _softmax_cols(sw, (dpos >= 0) & (dpos < WINDOW))
    o_win = _dot(vwt_ref[0, 0, :, pl.ds(start, wk)], p_win.astype(BF16))

    def gate(branch):
        return jnp.concatenate([g_ref[0, 0, branch * GROUP + r:branch * GROUP + r + 1, :] for r in range(GROUP)], axis=1)

    o = gate(0) * o_cmp + gate(1) * o_sel + gate(2) * o_win
    for r in range(GROUP):
        o_ref[0, 0, r] = o[:, r * tq:(r + 1) * tq]


def _prompt_attn(qT, gT, kc, vcT, ks, vsT, kw, vwT, mT, *, tq, kt_size):
    B, G, _, Dh, T = qT.shape
    NC = kc.shape[2]
    bg = lambda b, g, i: (b, g, 0, 0)
    return pl.pallas_call(
        functools.partial(_prompt_attn_kernel, tq=tq, kt_size=kt_size),
        grid=(B, G, T // tq),
        in_specs=[
            pl.BlockSpec((1, 1, GROUP, Dh, tq), lambda b, g, i: (b, g, 0, 0, i)),
            pl.BlockSpec((1, 1, gT.shape[2], tq), lambda b, g, i: (b, g, 0, i)),
            pl.BlockSpec((1, 1, NC, Dh), bg),
            pl.BlockSpec((1, 1, Dh, NC), bg),
            pl.BlockSpec((1, 1, T, Dh), bg),
            pl.BlockSpec((1, 1, Dh, T), bg),
            pl.BlockSpec((1, 1, T, Dh), bg),
            pl.BlockSpec((1, 1, Dh, T), bg),
            pl.BlockSpec(mT.shape, lambda b, g, i: (0, 0)),
        ],
        out_specs=pl.BlockSpec((1, 1, GROUP, Dh, tq), lambda b, g, i: (b, g, 0, 0, i)),
        out_shape=jax.ShapeDtypeStruct((B, G, GROUP, Dh, T), F32),
        scratch_shapes=[pltpu.VMEM((mT.shape[0], GROUP * tq), F32)],
        compiler_params=_cparams(("arbitrary", "arbitrary", "arbitrary")),
        name="prompt_attn",
    )(qT, gT, kc, vcT, ks, vsT, kw, vwT, mT)


def _dot_tn(v, p):
    return lax.dot_general(v, p, (((0,), (0,)), ((), ())), preferred_element_type=F32)


PAGES_PER_STEP = 8


def _page_specs(block_streams, stream_block, page):
    def spec(j):
        return pl.BlockSpec((1, block_streams, N_KV, HEAD_DIM, page),
                            lambda b, kt, pt: (pt[b, kt * PAGES_PER_STEP + j], stream_block, 0, 0, 0))
    return spec


def _sample_compress_kernel(pt_ref, *refs, page):
    del pt_ref
    pages = refs[:PAGES_PER_STEP]
    pe_ref, w1_ref, wp_ref, b1_ref, w2_ref, o_ref, panel, xs = refs[PAGES_PER_STEP:]
    kt = pl.program_id(1)
    cps = (page // CMP_STRIDE) * PAGES_PER_STEP
    row0 = pl.multiple_of(kt * cps, cps)
    for st in range(2):
        for gp in range(N_KV // 2):
            for j, p_ref in enumerate(pages):
                tile = p_ref[0, st, 2 * gp:2 * gp + 2].reshape(2 * HEAD_DIM, page)
                panel[j * page:(j + 1) * page, :] = tile.T
            for s in range(CMP_STRIDE):
                xs[st, gp, s, pl.ds(row0, cps), :] = panel[pl.ds(s, cps, stride=CMP_STRIDE), :].astype(BF16)

    @pl.when(kt == pl.num_programs(1) - 1)
    def _():
        for st in range(2):
            base = _dot(pe_ref[st], w1_ref[st])[0:1, :] + b1_ref[st]
            for gp in range(N_KV // 2):
                acc = None
                for sp in range(CMP_STRIDE // 2):
                    lhs = jnp.concatenate([xs[st, gp, 2 * sp], xs[st, gp, 2 * sp + 1]], axis=1)
                    d = _dot(lhs, wp_ref[st, sp])
                    acc = d if acc is None else acc + d
                for g2 in range(2):
                    parts = acc[:, g2 * 2 * CMP_HID:(g2 + 1) * 2 * CMP_HID]
                    o_ref[st, 0, 2 * gp + g2] = _compress_finish(parts, base, w2_ref[st])


def _sample_compress(cache_t, page_table, pe, w1, wp, b1, w2):
    page = cache_t.shape[-1]
    DB, n_pages = page_table.shape
    nchunk = n_pages * page // CMP_STRIDE
    assert n_pages % PAGES_PER_STEP == 0 and 2 * HEAD_DIM == 128 and page == 128
    const = lambda a: pl.BlockSpec(a.shape, lambda b, kt, pt: (0,) * a.ndim)
    spec = _page_specs(2, 0, page)
    return pl.pallas_call(
        functools.partial(_sample_compress_kernel, page=page),
        grid_spec=pltpu.PrefetchScalarGridSpec(
            num_scalar_prefetch=1,
            grid=(DB, n_pages // PAGES_PER_STEP),
            in_specs=[spec(j) for j in range(PAGES_PER_STEP)] + [const(pe), const(w1), const(wp), const(b1), const(w2)],
            out_specs=pl.BlockSpec((2, 1, N_KV, nchunk, HEAD_DIM), lambda b, kt, pt: (0, b, 0, 0, 0)),
            scratch_shapes=[pltpu.VMEM((PAGES_PER_STEP * page, 2 * HEAD_DIM), F32),
                            pltpu.VMEM((2, N_KV // 2, CMP_STRIDE, nchunk, 2 * HEAD_DIM), BF16)],
        ),
        out_shape=jax.ShapeDtypeStruct((2, DB, N_KV, nchunk, HEAD_DIM), BF16),
        compiler_params=_cparams(("arbitrary", "arbitrary")),
        name="sample_compress",
    )(page_table, *([cache_t] * PAGES_PER_STEP), pe, w1, wp, b1, w2)


def _sample_attn_kernel(pt_ref, *refs, t_new, past, page):
    del pt_ref
    pages = refs[:PAGES_PER_STEP]
    (q_ref, g_ref, kc_ref, vc_ref, kn_ref, vn_ref, kw_ref, vw_ref, kwn_ref, vwn_ref, mt_ref, rr_ref,
     o_ref, bias_ref, m_ref, l_ref, acc_ref, oc_ref) = refs[PAGES_PER_STEP:]
    kt = pl.program_id(1)
    gd = N_KV * HEAD_DIM
    R = N_KV * GROUP * t_new
    qbd = q_ref[0]
    lane = lax.broadcasted_iota(jnp.int32, (1, R), 1)
    i_lane = lane % t_new
    t_lane = past + i_lane

    @pl.when(kt == 0)
    def _():
        n_cmp = kc_ref.shape[2]
        sc = None
        for g in range(N_KV):
            d = _dot(kc_ref[0, g], qbd[g * HEAD_DIM:(g + 1) * HEAD_DIM, :])
            sc = d if sc is None else sc + d
        n_idx = lax.broadcasted_iota(jnp.int32, (n_cmp, R), 0)
        p_cmp = _softmax_cols(sc, n_idx * CMP_STRIDE + (CMP_BLOCK - 1) <= t_lane)
        pb = p_cmp.astype(BF16)
        o_cmp = jnp.concatenate([_dot_tn(vc_ref[0, g], pb) for g in range(N_KV)], axis=0)

        hi = lax.Precision.HIGHEST
        imp = jnp.dot(p_cmp, rr_ref[...], precision=hi, preferred_element_type=F32)
        selimp = jnp.dot(mt_ref[...], imp, precision=hi, preferred_element_type=F32)
        sel = _select_blocks(selimp, t_lane // SEL_BLOCK, N_SELECT)
        bias_ref[...] = jnp.where(sel, 0.0, NEG)

        wbuf = kw_ref.shape[-1]
        sw = jnp.concatenate([_dot_tn(kw_ref[0].astype(BF16), qbd), _dot(kwn_ref[0], qbd)], axis=0)
        jw = lax.broadcasted_iota(jnp.int32, (wbuf + t_new, R), 0)
        dpos = wbuf + i_lane - jw
        p_win = _softmax_cols(sw, (dpos >= 0) & (dpos < WINDOW) & (past - wbuf + jw >= 0)).astype(BF16)
        o_win = _dot(vw_ref[0].astype(BF16), p_win[0:wbuf]) + _dot_tn(vwn_ref[0], p_win[wbuf:wbuf + t_new])
        oc_ref[...] = g_ref[0, 0:1, :] * o_cmp + g_ref[0, 2:3, :] * o_win

        new_blk = past // SEL_BLOCK
        s_new = _dot(kn_ref[0], qbd) + bias_ref[new_blk:new_blk + 1, :]
        ip = lax.broadcasted_iota(jnp.int32, (t_new, R), 0)
        s_new = jnp.where(ip <= i_lane, s_new, NEG)
        m0 = jnp.max(s_new, axis=0, keepdims=True)
        p0 = jnp.exp(s_new - m0)
        m_ref[...] = m0
        l_ref[...] = jnp.sum(p0, axis=0, keepdims=True)
        acc_ref[...] = _dot_tn(vn_ref[0], p0.astype(BF16))

    blk_per_page = page // SEL_BLOCK
    nb = PAGES_PER_STEP * blk_per_page
    b_t = bias_ref[pl.ds(pl.multiple_of(kt * nb, nb), nb), :]
    scores = []
    for j, p_ref in enumerate(pages):
        s = _dot_tn(p_ref[0, 0].reshape(gd, page).astype(BF16), qbd)
        scores.append(jnp.concatenate(
            [s[jb * SEL_BLOCK:(jb + 1) * SEL_BLOCK] + b_t[j * blk_per_page + jb:j * blk_per_page + jb + 1, :]
             for jb in range(blk_per_page)], axis=0))
    m_old = m_ref[...]
    m_new = m_old
    for s in scores:
        m_new = jnp.maximum(m_new, jnp.max(s, axis=0, keepdims=True))
    alpha = jnp.exp(m_old - m_new)
    l = alpha * l_ref[...]
    acc = alpha * acc_ref[...]
    for s, p_ref in zip(scores, pages):
        p = jnp.exp(s - m_new)
        l = l + jnp.sum(p, axis=0, keepdims=True)
        acc = acc + _dot(p_ref[0, 1].reshape(gd, page).astype(BF16), p.astype(BF16))
    m_ref[...] = m_new
    l_ref[...] = l
    acc_ref[...] = acc

    @pl.when(kt == pl.num_programs(1) - 1)
    def _():
        o = oc_ref[...] + g_ref[0, 1:2, :] * (acc / jnp.maximum(l, 1e-30))
        grp = lane // (GROUP * t_new)
        out = jnp.zeros((HEAD_DIM, R), F32)
        for g in range(N_KV):
            out = out + jnp.where(grp == g, o[g * HEAD_DIM:(g + 1) * HEAD_DIM, :], 0.0)
        o_ref[0] = out


def _sample_attn(cache_t, page_table, qbd, gT, kcv, kn, vn, kw, vw, kwn, vwn, mT, rr, *, t_new):
    page = cache_t.shape[-1]
    DB, n_pages = page_table.shape
    past = n_pages * page
    R = qbd.shape[-1]
    per_b = lambda a: pl.BlockSpec((1,) + a.shape[1:], lambda b, kt, pt: (b,) + (0,) * (a.ndim - 1))
    const = lambda a: pl.BlockSpec(a.shape, lambda b, kt, pt: (0,) * a.ndim)
    cmp_spec = lambda st: pl.BlockSpec((None, 1) + kcv.shape[2:], lambda b, kt, pt: (st, b, 0, 0, 0))
    spec = _page_specs(2, 1, page)
    return pl.pallas_call(
        functools.partial(_sample_attn_kernel, t_new=t_new, past=past, page=page),
        grid_spec=pltpu.PrefetchScalarGridSpec(
            num_scalar_prefetch=1,
            grid=(DB, n_pages // PAGES_PER_STEP),
            in_specs=[spec(j) for j in range(PAGES_PER_STEP)]
            + [per_b(qbd), per_b(gT), cmp_spec(0), cmp_spec(1), per_b(kn), per_b(vn), per_b(kw), per_b(vw),
               per_b(kwn), per_b(vwn), const(mT), const(rr)],
            out_specs=pl.BlockSpec((1, HEAD_DIM, R), lambda b, kt, pt: (b, 0, 0)),
            scratch_shapes=[pltpu.VMEM((mT.shape[0], R), F32), pltpu.VMEM((1, R), F32), pltpu.VMEM((1, R), F32),
                            pltpu.VMEM((N_KV * HEAD_DIM, R), F32), pltpu.VMEM((N_KV * HEAD_DIM, R), F32)],
        ),
        out_shape=jax.ShapeDtypeStruct((DB, HEAD_DIM, R), F32),
        compiler_params=_cparams(("arbitrary", "arbitrary")),
        name="sample_attn",
    )(page_table, *([cache_t] * PAGES_PER_STEP), qbd, gT, kcv, kcv, kn, vn, kw, vw, kwn, vwn, mT, rr)


def _nsa_outproj_kernel(o_ref, sz_ref, x_ref, w_ref, g_ref, b_ref, y_ref):
    gated = (o_ref[...] * sz_ref[...]).astype(BF16)
    y = _dot(gated, w_ref[...])
    y_ref[...] = _layer_norm(ALPHA * x_ref[...] + y, g_ref[...], b_ref[...])


def _nsa_outproj(o, sz, x, w, g, b, *, tm):
    N, D = x.shape
    row = lambda i: (i, 0)
    const = lambda i: (0, 0)
    return pl.pallas_call(
        _nsa_outproj_kernel,
        grid=(N // tm,),
        in_specs=[pl.BlockSpec((tm, o.shape[1]), row), pl.BlockSpec((tm, sz.shape[1]), row),
                  pl.BlockSpec((tm, D), row), pl.BlockSpec(w.shape, const),
                  pl.BlockSpec(g.shape, const), pl.BlockSpec(b.shape, const)],
        out_specs=pl.BlockSpec((tm, D), row),
        out_shape=jax.ShapeDtypeStruct((N, D), F32),
        compiler_params=_cparams(("arbitrary",)),
        name="nsa_outproj",
    )(o, sz, x, w, g, b)


def _prep_nsa_weights(w_in, pe, w1, b1, w2):
    D = w_in.shape[0]
    att_w = N_KV * GROUP * HEAD_DIM
    kv_w = 6 * N_KV * HEAD_DIM
    o1, o2 = att_w, att_w + kv_w
    o3 = o2 + 3 * N_KV * GROUP
    w_q = w_in[:, :o1] * (HEAD_DIM ** -0.5)
    w_g = w_in[:, o2:o3].reshape(D, N_KV, GROUP, 3).transpose(0, 3, 1, 2).reshape(D, 3 * N_KV * GROUP)
    w_g = jnp.pad(w_g, ((0, 0), (0, 128 - w_g.shape[1])))
    w_all = jnp.concatenate([w_q, w_in[:, o1:o2], w_in[:, o3:], w_g], axis=1).astype(BF16)
    r = CMP_BLOCK // CMP_STRIDE
    pe_rows = jnp.pad(pe.reshape(2, 1, CMP_BLOCK * HEAD_DIM), ((0, 0), (0, 7), (0, 0))).astype(BF16)
    w1_flat = w1.reshape(2, CMP_BLOCK * HEAD_DIM, CMP_HID).astype(BF16)
    w1r = w1.reshape(2, r, CMP_STRIDE * HEAD_DIM, CMP_HID)
    w1_split = jnp.concatenate([w1r[:, j] for j in range(r)], axis=-1).astype(BF16)
    w1p = w1.reshape(2, r, CMP_STRIDE // 2, 2, HEAD_DIM, CMP_HID).transpose(0, 2, 3, 4, 1, 5)
    w1p = w1p.reshape(2, CMP_STRIDE // 2, 2, HEAD_DIM, r * CMP_HID)
    eye = jnp.eye(2, dtype=w1.dtype)
    w1_pair = (w1p[:, :, :, None, :, None, :] * eye[None, None, None, :, None, :, None])
    w1_pair = w1_pair.reshape(2, CMP_STRIDE // 2, 4 * HEAD_DIM, 2 * r * CMP_HID).astype(BF16)
    return w_all, pe_rows, w1_flat, w1_split, w1_pair, b1.reshape(2, 1, CMP_HID), w2.astype(BF16)


def _nsa_prompt(x1, nsa_w, w_out, g, b):
    B, T, D = x1.shape
    w_all, pe_rows, w1_flat, w1_split, w1_pair, b1, w2 = nsa_w
    att_w = N_KV * GROUP * HEAD_DIM
    kv_w = 6 * N_KV * HEAD_DIM
    x_flat = x1.reshape(B * T, D)
    q, kv, sz, gates = _nsa_inproj(x_flat, w_all, d_q=att_w, d_kv=kv_w, tm=512)
    kv6 = kv.reshape(B, T, 6, N_KV, HEAD_DIM)
    rows = kv6[:, :, :4]
    win = kv6[:, T - min(WINDOW, T):, 4:]
    qT = q.reshape(B, T, N_KV, GROUP, HEAD_DIM).transpose(0, 2, 3, 4, 1)
    gT = gates[:, :3 * N_KV * GROUP].reshape(B, T, 3, N_KV, GROUP).transpose(0, 3, 2, 4, 1)
    gT = jnp.pad(gT.reshape(B, N_KV, 3 * GROUP, T), ((0, 0), (0, 0), (0, 16 - 3 * GROUP), (0, 0)))
    kvb = kv6.astype(BF16).transpose(2, 0, 3, 1, 4)
    nchunk = T // CMP_STRIDE
    x_cmp = kvb[0:2].reshape(2, B * N_KV, nchunk, CMP_STRIDE * HEAD_DIM)
    kcv = _compress(x_cmp, pe_rows, w1_flat, w1_split, b1, w2).reshape(2, B, N_KV, nchunk, HEAD_DIM)
    mT = jnp.asarray(_cmp_to_sel_matrix(T // SEL_BLOCK, nchunk))
    tr = lambda a: a.transpose(0, 1, 3, 2)
    oT = _prompt_attn(qT, gT, kcv[0], tr(kcv[1]), kvb[2], tr(kvb[3]), kvb[4], tr(kvb[5]), mT, tq=128, kt_size=512)
    o = oT.transpose(0, 4, 1, 2, 3).reshape(B * T, att_w)
    y = _nsa_outproj(o, sz, x_flat, w_out, g, b, tm=512)
    return y.reshape(B, T, D), rows, win


def _nsa_sample(x1, cache_kv_l, cache_win_l, page_table, nsa_w, w_out, g, b):
    DB, T, D = x1.shape
    w_all, pe_rows, w1_flat, w1_split, w1_pair, b1, w2 = nsa_w
    att_w = N_KV * GROUP * HEAD_DIM
    kv_w = 6 * N_KV * HEAD_DIM
    page = cache_kv_l.shape[1]
    past = page_table.shape[1] * page
    assert T <= CMP_STRIDE and past % SEL_BLOCK == 0 and past % CMP_STRIDE == 0
    x_flat = x1.reshape(DB * T, D)
    q, kv, sz, gates = _nsa_inproj(x_flat, w_all, d_q=att_w, d_kv=kv_w, tm=DB * T)
    kv6 = kv.reshape(DB, T, 6, N_KV, HEAD_DIM)
    rows = kv6[:, :, :4]
    wseq = jnp.concatenate([cache_win_l, kv6[:, :, 4:]], axis=1)
    gd = N_KV * HEAD_DIM
    R = N_KV * GROUP * T
    qT = q.reshape(DB, T, N_KV, GROUP, HEAD_DIM).transpose(0, 2, 4, 3, 1)
    eye = jnp.eye(N_KV, dtype=BF16)
    qbd = (qT[:, :, :, None] * eye[None, :, None, :, None, None]).reshape(DB, gd, R)
    gT = gates[:, :3 * N_KV * GROUP].reshape(DB, T, 3, N_KV * GROUP).transpose(0, 2, 3, 1).reshape(DB, 3, R)
    gT = jnp.pad(gT, ((0, 0), (0, 5), (0, 0)))
    new_rows = lambda st: kv6[:, :, st].reshape(DB, T, gd).astype(BF16)
    cache_t = cache_kv_l.transpose(0, 2, 3, 4, 1)
    win_t = cache_win_l.transpose(0, 2, 3, 4, 1)
    wbuf = win_t.shape[-1]
    nchunk = past // CMP_STRIDE
    kcv = _sample_compress(cache_t, page_table, pe_rows, w1_flat, w1_pair, b1, w2)
    n_sel = past // SEL_BLOCK + 1
    n_sel_pad = -(-n_sel // (PAGES_PER_STEP * page // SEL_BLOCK)) * (PAGES_PER_STEP * page // SEL_BLOCK)
    mT = np.zeros((n_sel_pad, nchunk), np.float32)
    mT[:n_sel] = _cmp_to_sel_matrix(n_sel, nchunk)
    lane = np.arange(R)
    rr = ((lane[:, None] % T == lane[None, :] % T)
          & (lane[:, None] // (GROUP * T) == lane[None, :] // (GROUP * T))).astype(np.float32)
    oT = _sample_attn(cache_t, page_table, qbd, gT, kcv, new_rows(2), new_rows(3),
                      win_t[:, 0].reshape(DB, gd, wbuf), win_t[:, 1].reshape(DB, gd, wbuf), new_rows(4), new_rows(5),
                      jnp.asarray(mT), jnp.asarray(rr), t_new=T)
    o = oT.reshape(DB, HEAD_DIM, N_KV, GROUP, T).transpose(0, 4, 2, 3, 1).reshape(DB * T, att_w)
    y = _nsa_outproj(o, sz, x_flat, w_out, g, b, tm=DB * T)
    return y.reshape(DB, T, D), rows, wseq[:, T:]


def kernel(x_prompt, x_sample, state_conv, cache_kv, cache_win, page_table, conv_w_in, conv_b_in, conv_dw_w,
           conv_dw_b, conv_ln_g, conv_ln_b, conv_w_out, nsa_w_in, cmp_pe, cmp_w1, cmp_b1, cmp_w2, nsa_w_out,
           ln_g, ln_b):
    B, T, D = x_prompt.shape
    DB, TS, _ = x_sample.shape
    d_in = conv_w_out.shape[1]
    row = lambda v: v.reshape(1, -1)

    cw = (conv_w_in[0].astype(BF16), row(conv_b_in[0]), conv_dw_w[0], row(conv_dw_b[0]), row(conv_ln_g[0]),
          row(conv_ln_b[0]), conv_w_out[0].astype(BF16), row(ln_g[0]), row(ln_b[0]))
    pad = HALO - (CONV_W - 1)
    xp, sp = _conv_layer(x_prompt, jnp.zeros((B, HALO, d_in), F32), *cw, ts=256, carry=True)
    st = jnp.pad(state_conv[0], ((0, 0), (pad, 0), (0, 0)))
    xs, ss = _conv_layer(x_sample, st, *cw, ts=TS, carry=False)

    nsa_w = _prep_nsa_weights(nsa_w_in[0], cmp_pe[0], cmp_w1[0], cmp_b1[0], cmp_w2[0])
    w_out = nsa_w_out[0].astype(BF16)
    yp, rp, wp = _nsa_prompt(xp, nsa_w, w_out, row(ln_g[1]), row(ln_b[1]))
    ys, rs, ws = _nsa_sample(xs, cache_kv[0], cache_win[0], page_table, nsa_w, w_out, row(ln_g[1]), row(ln_b[1]))
    return (yp, ys, sp[None, :, pad:], ss[None, :, pad:], rp[None], rs[None], wp[None], ws[None])
```

```python
import functools

import jax
import jax.numpy as jnp
import numpy as np
from jax import lax
from jax.experimental import pallas as pl
from jax.experimental.pallas import tpu as pltpu

F32 = jnp.float32
BF16 = jnp.bfloat16

CONV_W = 31
N_KV = 4
GROUP = 4
HEAD_DIM = 64
CMP_BLOCK = 32
CMP_STRIDE = 16
CMP_HID = 128
SEL_BLOCK = 64
N_SELECT = 16
WINDOW = 512
DEPTH = 2
ALPHA = (2 * DEPTH) ** 0.25
LN_EPS = 1e-5
NEG = -1e30

VMEM_LIMIT = 56 * 1024 * 1024
LANES = 128
HALO = 32


def _cparams(sem):
    return pltpu.CompilerParams(dimension_semantics=sem, vmem_limit_bytes=VMEM_LIMIT)


def _sigmoid(x):
    return 1.0 / (1.0 + jnp.exp(-x))


def _layer_norm(x, g, b):
    mu = jnp.mean(x, axis=-1, keepdims=True)
    xc = x - mu
    var = jnp.mean(xc * xc, axis=-1, keepdims=True)
    return xc * lax.rsqrt(var + LN_EPS) * g + b


def _dot(a, b):
    return jnp.dot(a, b, preferred_element_type=F32)


def _conv_layer_kernel(x_ref, st_ref, win_ref, bin_ref, dww_ref, dwb_ref, clg_ref, clb_ref, wout_ref,
                       lng_ref, lnb_ref, y_ref, ns_ref, ubuf, zbuf, cbuf, *, ns, ts, d_in, carry):
    t = pl.program_id(1)
    rows = ns * ts
    cb = LANES
    n_cb = d_in // cb
    lanes = lambda ci: slice(ci * cb, (ci + 1) * cb)

    if carry:
        @pl.when(t == 0)
        def _():
            ubuf[:, :, 0:HALO, :] = jnp.zeros((ns, n_cb, HALO, cb), F32)
    else:
        for ci in range(n_cb):
            ubuf[:, ci, 0:HALO, :] = st_ref[:, :, lanes(ci)]

    x = x_ref[...].reshape(rows, x_ref.shape[-1])
    xb = x.astype(BF16)
    a = _dot(xb, win_ref[:, 0:d_in]) + bin_ref[:, 0:d_in]
    gl = _dot(xb, win_ref[:, d_in:2 * d_in]) + bin_ref[:, d_in:2 * d_in]
    u = (a * _sigmoid(gl)).reshape(ns, ts, d_in)
    for ci in range(n_cb):
        ubuf[:, ci, HALO:HALO + ts, :] = u[:, :, lanes(ci)]
    z = _dot(xb, win_ref[:, 2 * d_in:3 * d_in]) + bin_ref[:, 2 * d_in:3 * d_in]
    zbuf[...] = z * _sigmoid(z)

    rb = min(ts, 64)
    n_rb = ts // rb
    off0 = HALO - (CONV_W - 1)

    def chunk(i, c):
        s = i // (n_rb * n_cb)
        rem = i % (n_rb * n_cb)
        r0 = pl.multiple_of((rem // n_cb) * rb, rb)
        ci = rem % n_cb
        c0 = pl.multiple_of(ci * cb, cb)
        acc = jnp.zeros((rb, cb), F32) + dwb_ref[:, pl.ds(c0, cb)]
        for k in range(CONV_W):
            acc = acc + dww_ref[k:k + 1, pl.ds(c0, cb)] * ubuf[s, ci, pl.ds(r0 + off0 + k, rb, stride=1), :]
        cbuf[s, pl.ds(r0, rb), pl.ds(c0, cb)] = acc
        return c

    lax.fori_loop(0, ns * n_rb * n_cb, chunk, 0)

    for ci in range(n_cb):
        ns_ref[:, :, lanes(ci)] = ubuf[:, ci, ts:ts + HALO, :]
    if carry:
        ubuf[:, :, 0:HALO, :] = ubuf[:, :, ts:ts + HALO, :]

    yc = _layer_norm(cbuf[...].reshape(rows, d_in), clg_ref[...], clb_ref[...])
    gated = (yc * _sigmoid(yc)) * zbuf[...]
    out = _dot(gated.astype(BF16), wout_ref[...])
    y = _layer_norm(ALPHA * x + out, lng_ref[...], lnb_ref[...])
    y_ref[...] = y.reshape(y_ref.shape)


def _conv_layer(x, state, w_in, b_in, dw_w, dw_b, cl_g, cl_b, w_out, ln_g, ln_b, *, ts, carry):
    S, T, D = x.shape
    d_in = w_out.shape[0]
    ns = 1 if carry else S
    grid = (S // ns, T // ts)
    const = lambda s, t: (0, 0)
    kernel = functools.partial(_conv_layer_kernel, ns=ns, ts=ts, d_in=d_in, carry=carry)
    y, new_state = pl.pallas_call(
        kernel,
        grid=grid,
        in_specs=[
            pl.BlockSpec((ns, ts, D), lambda s, t: (s, t, 0)),
            pl.BlockSpec((ns, HALO, d_in), lambda s, t: (s, 0, 0)),
            pl.BlockSpec(w_in.shape, const),
            pl.BlockSpec(b_in.shape, const),
            pl.BlockSpec(dw_w.shape, const),
            pl.BlockSpec(dw_b.shape, const),
            pl.BlockSpec(cl_g.shape, const),
            pl.BlockSpec(cl_b.shape, const),
            pl.BlockSpec(w_out.shape, const),
            pl.BlockSpec(ln_g.shape, const),
            pl.BlockSpec(ln_b.shape, const),
        ],
        out_specs=[
            pl.BlockSpec((ns, ts, D), lambda s, t: (s, t, 0)),
            pl.BlockSpec((ns, HALO, d_in), lambda s, t: (s, 0, 0)),
        ],
        out_shape=[
            jax.ShapeDtypeStruct((S, T, D), F32),
            jax.ShapeDtypeStruct((S, HALO, d_in), F32),
        ],
        scratch_shapes=[
            pltpu.VMEM((ns, d_in // LANES, HALO + ts, LANES), F32),
            pltpu.VMEM((ns * ts, d_in), F32),
            pltpu.VMEM((ns, ts, d_in), F32),
        ],
        compiler_params=_cparams(("arbitrary", "arbitrary")),
        name="conv_layer_carry" if carry else "conv_layer_state",
    )(x, state, w_in, b_in, dw_w, dw_b, cl_g, cl_b, w_out, ln_g, ln_b)
    return y, new_state


def _nsa_inproj_kernel(x_ref, w_ref, q_ref, kv_ref, sz_ref, g_ref, *, d_q, d_kv):
    xb = x_ref[...].astype(BF16)
    q_ref[...] = _dot(xb, w_ref[:, 0:d_q]).astype(BF16)
    kv_ref[...] = _dot(xb, w_ref[:, d_q:d_q + d_kv])
    z = _dot(xb, w_ref[:, d_q + d_kv:2 * d_q + d_kv])
    sz_ref[...] = z * _sigmoid(z)
    g_ref[...] = _sigmoid(_dot(xb, w_ref[:, 2 * d_q + d_kv:]))


def _nsa_inproj(x, w, *, d_q, d_kv, tm):
    N, D = x.shape
    d_g = w.shape[1] - 2 * d_q - d_kv
    row = lambda i: (i, 0)
    return pl.pallas_call(
        functools.partial(_nsa_inproj_kernel, d_q=d_q, d_kv=d_kv),
        grid=(N // tm,),
        in_specs=[pl.BlockSpec((tm, D), row), pl.BlockSpec(w.shape, lambda i: (0, 0))],
        out_specs=[pl.BlockSpec((tm, d_q), row), pl.BlockSpec((tm, d_kv), row),
                   pl.BlockSpec((tm, d_q), row), pl.BlockSpec((tm, d_g), row)],
        out_shape=[jax.ShapeDtypeStruct((N, d_q), BF16), jax.ShapeDtypeStruct((N, d_kv), F32),
                   jax.ShapeDtypeStruct((N, d_q), F32), jax.ShapeDtypeStruct((N, d_g), F32)],
        compiler_params=_cparams(("arbitrary",)),
        name="nsa_inproj",
    )(x, w)


def _compress_finish(parts, base, w2):
    nchunk = parts.shape[0]
    second = pltpu.roll(parts[:, CMP_HID:], nchunk - 1, 0)
    row = lax.broadcasted_iota(jnp.int32, (nchunk, CMP_HID), 0)
    second = jnp.where(row < nchunk - 1, second, 0.0)
    h = base + parts[:, :CMP_HID] + second
    act = h * _sigmoid(h)
    return _dot(act.astype(BF16), w2).astype(BF16)


def _compress_kernel(x_ref, pe_ref, w1_ref, w1s_ref, b1_ref, w2_ref, o_ref):
    parts = _dot(x_ref[0, 0], w1s_ref[0])
    base = _dot(pe_ref[0], w1_ref[0])[0:1, :] + b1_ref[0]
    o_ref[0, 0] = _compress_finish(parts, base, w2_ref[0])


def _compress(x, pe, w1, w1s, b1, w2):
    n_st, BG, nchunk, K = x.shape
    st = lambda s, i: (s, 0, 0)
    return pl.pallas_call(
        _compress_kernel,
        grid=(n_st, BG),
        in_specs=[
            pl.BlockSpec((1, 1, nchunk, K), lambda s, i: (s, i, 0, 0)),
            pl.BlockSpec((1,) + pe.shape[1:], st),
            pl.BlockSpec((1,) + w1.shape[1:], st),
            pl.BlockSpec((1,) + w1s.shape[1:], st),
            pl.BlockSpec((1,) + b1.shape[1:], st),
            pl.BlockSpec((1,) + w2.shape[1:], st),
        ],
        out_specs=pl.BlockSpec((1, 1, nchunk, HEAD_DIM), lambda s, i: (s, i, 0, 0)),
        out_shape=jax.ShapeDtypeStruct((n_st, BG, nchunk, HEAD_DIM), BF16),
        compiler_params=_cparams(("arbitrary", "arbitrary")),
        name="compress",
    )(x, pe, w1, w1s, b1, w2)


def _softmax_cols(s, valid):
    s = jnp.where(valid, s, NEG)
    m = jnp.max(s, axis=0, keepdims=True)
    e = jnp.where(valid, jnp.exp2(s - m), 0.0)
    return e / jnp.maximum(jnp.sum(e, axis=0, keepdims=True), 1e-30)


def _selection_scores(selimp, cur):
    j = lax.broadcasted_iota(jnp.int32, selimp.shape, 0)
    causal = j <= cur
    forced = (j == 0) | (j == cur) | (j == cur - 1)
    return jnp.where(forced, jnp.inf, jnp.where(causal, selimp, -jnp.inf)), causal


def _select_blocks(selimp, cur, n_select):
    ns = selimp.shape[0]
    j = lax.broadcasted_iota(jnp.int32, selimp.shape, 0)
    score, causal = _selection_scores(selimp, cur)
    taken = jnp.zeros(selimp.shape, jnp.int32)
    for _ in range(n_select):
        m = jnp.max(score, axis=0, keepdims=True)
        first = jnp.min(jnp.where(score == m, j, ns), axis=0, keepdims=True)
        hit = j == first
        taken = jnp.where(hit, 1, taken)
        score = jnp.where(hit, -jnp.inf, score)
    return (taken > 0) & causal


def _select_blocks_by_rank(selimp, cur, n_select, score_ref):
    ns = selimp.shape[0]
    score, causal = _selection_scores(selimp, cur)
    score_ref[...] = score
    sub = lax.broadcasted_iota(jnp.int32, (8, selimp.shape[1]), 0)
    ranks = []
    for v in range(ns // 8):
        blk = score[8 * v:8 * v + 8]
        rank = jnp.zeros(blk.shape, F32)
        for jp in range(ns):
            row = score_ref[jp:jp + 1, :]
            if jp < 8 * v:
                ahead = row >= blk
            elif jp >= 8 * v + 8:
                ahead = row > blk
            else:
                ahead = (row > blk) | ((row == blk) & (sub > jp - 8 * v))
            rank = rank + jnp.where(ahead, 1.0, 0.0)
        ranks.append(rank)
    return (jnp.concatenate(ranks, axis=0) < n_select) & causal


def _cmp_to_sel_matrix(n_sel, n_cmp):
    r = SEL_BLOCK // CMP_STRIDE
    lo = -(CMP_BLOCK // CMP_STRIDE - 1)
    m = np.zeros((n_sel, n_cmp), np.float32)
    for o in range(lo, r):
        start = o * CMP_STRIDE
        ov = max(0, min(SEL_BLOCK, start + CMP_BLOCK) - max(0, start))
        for jb in range(n_sel):
            n = r * jb + o
            if ov > 0 and 0 <= n < n_cmp:
                m[jb, n] += ov / CMP_BLOCK
    return m


V_AUG = 80
K_AUG = 128
BIAS_ROWS = 16
PROMPT_GROUPS_PER_STEP = 4


def _prompt_attn_kernel(q_ref, g_ref, kc_ref, vct_ref, ksa_ref, vsa_ref, kw_ref, vwa_ref, mt_ref, o_ref,
                        bias_ref, score_ref, *, tq, kt_size, gps):
    qi = pl.program_id(2)
    qs = qi * tq
    hp = GROUP // 2
    RH = 2 * tq
    n_cmp = kc_ref.shape[2]
    n_sel = mt_ref.shape[0]
    blk_per_tile = kt_size // SEL_BLOCK
    chains = [(gi, h) for gi in range(gps) for h in range(hp)]

    t1 = qs + lax.broadcasted_iota(jnp.int32, (1, tq), 1)
    t_lane = jnp.concatenate([t1, t1], axis=1)
    qT = {(gi, h): jnp.concatenate([q_ref[0, gi, 2 * h], q_ref[0, gi, 2 * h + 1]], axis=1) for gi, h in chains}

    wk = WINDOW + tq
    start = pl.multiple_of(jnp.maximum(qs - WINDOW, 0), 128)
    s_cmp = {c: _dot(kc_ref[0, c[0]], qT[c]) for c in chains}
    s_win = {c: _dot(kw_ref[0, c[0], pl.ds(start, wk), :], qT[c]) for c in chains}

    n_idx = lax.broadcasted_iota(jnp.int32, (n_cmp, RH), 0)
    cmp_valid = n_idx * CMP_STRIDE + (CMP_BLOCK - 1) <= t_lane
    o_cmp = {}
    imps = [None] * gps
    for gi, h in chains:
        p_cmp = _softmax_cols(s_cmp[gi, h], cmp_valid)
        o_cmp[gi, h] = _dot(vct_ref[0, gi], p_cmp.astype(BF16))
        part = p_cmp[:, 0:tq] + p_cmp[:, tq:2 * tq]
        imps[gi] = part if imps[gi] is None else imps[gi] + part

    dpos = t_lane - (start + lax.broadcasted_iota(jnp.int32, (wk, RH), 0))
    win_valid = (dpos >= 0) & (dpos < WINDOW)
    o_win = {}
    for gi, h in chains:
        sw = jnp.where(win_valid, s_win[gi, h], NEG)
        p_win = jnp.exp2(sw - jnp.max(sw, axis=0, keepdims=True)).astype(BF16)
        acc_win = _dot(vwa_ref[0, gi, :, pl.ds(start, wk)], p_win)
        o_win[gi, h] = acc_win[0:HEAD_DIM] / jnp.maximum(acc_win[HEAD_DIM:HEAD_DIM + 1], 1e-30)

    pad = jnp.zeros((BIAS_ROWS - blk_per_tile, tq), F32)
    for gi in range(gps):
        selimp = jnp.dot(mt_ref[...], imps[gi], precision=lax.Precision.HIGHEST, preferred_element_type=F32)
        sel = _select_blocks_by_rank(selimp, t1 // SEL_BLOCK, N_SELECT, score_ref.at[gi])
        bias = jnp.where(sel, 0.0, NEG)
        rows = []
        for kt in range(n_sel // blk_per_tile):
            rows += [bias[kt * blk_per_tile:(kt + 1) * blk_per_tile], pad]
        bias_rows = jnp.concatenate(rows, axis=0)
        bias_ref[gi] = jnp.concatenate([bias_rows, bias_rows], axis=1).astype(BF16)

    q_pad = jnp.zeros((K_AUG - HEAD_DIM - BIAS_ROWS, RH), BF16)

    def sel_units(kt, carries, diagonal):
        k0 = pl.multiple_of(kt * kt_size, kt_size)
        scores = []
        for gi, h in chains:
            b_t = bias_ref[gi, pl.ds(pl.multiple_of(kt * BIAS_ROWS, BIAS_ROWS), BIAS_ROWS), :]
            qa = jnp.concatenate([qT[gi, h], b_t, q_pad], axis=0)
            scores.append(_dot(ksa_ref[0, gi, pl.ds(k0, kt_size), :], qa))
        out = []
        for (gi, h), s, (m, acc) in zip(chains, scores, carries):
            if diagonal:
                kidx = k0 + lax.broadcasted_iota(jnp.int32, (kt_size, RH), 0)
                s = jnp.where(kidx <= t_lane, s, NEG)
            m_new = jnp.maximum(m, jnp.max(s, axis=0, keepdims=True))
            p = jnp.exp2(s - m_new).astype(BF16)
            acc = jnp.exp2(m - m_new) * acc + _dot(vsa_ref[0, gi, :, pl.ds(k0, kt_size)], p)
            out.append((m_new, acc))
        return tuple(out)

    n_full = qs // kt_size
    init = tuple((jnp.full((1, RH), NEG, F32), jnp.zeros((V_AUG, RH), F32)) for _ in chains)
    carries = lax.fori_loop(0, n_full, lambda kt, c: sel_units(kt, c, False), init)
    carries = sel_units(n_full, carries, True)

    def gate(gi, h, branch):
        r0 = branch * GROUP + 2 * h
        return jnp.concatenate([g_ref[0, gi, r0:r0 + 1, :], g_ref[0, gi, r0 + 1:r0 + 2, :]], axis=1)

    for i, (gi, h) in enumerate(chains):
        acc_sel = carries[i][1]
        o_sel = acc_sel[0:HEAD_DIM] / jnp.maximum(acc_sel[HEAD_DIM:HEAD_DIM + 1], 1e-30)
        o = gate(gi, h, 0) * o_cmp[gi, h] + gate(gi, h, 1) * o_sel + gate(gi, h, 2) * o_win[gi, h]
        o_ref[0, gi, 2 * h] = o[:, 0:tq]
        o_ref[0, gi, 2 * h + 1] = o[:, tq:2 * tq]


def _prompt_attn(qT, gT, kc, vcT, ksa, vsa, kw, vwa, mT, *, tq, kt_size, gps):
    B, G, _, Dh, T = qT.shape
    NC = kc.shape[2]
    n_sel = mT.shape[0]
    assert kt_size % SEL_BLOCK == 0 and kt_size // SEL_BLOCK <= BIAS_ROWS and T % kt_size == 0 and kt_size % tq == 0
    assert G % gps == 0
    bg = lambda b, g, i: (b, g, 0, 0)
    return pl.pallas_call(
        functools.partial(_prompt_attn_kernel, tq=tq, kt_size=kt_size, gps=gps),
        grid=(B, G // gps, T // tq),
        in_specs=[
            pl.BlockSpec((1, gps, GROUP, Dh, tq), lambda b, g, i: (b, g, 0, 0, i)),
            pl.BlockSpec((1, gps, gT.shape[2], tq), lambda b, g, i: (b, g, 0, i)),
            pl.BlockSpec((1, gps, NC, Dh), bg),
            pl.BlockSpec((1, gps, Dh, NC), bg),
            pl.BlockSpec((1, gps, T, K_AUG), bg),
            pl.BlockSpec((1, gps, V_AUG, T), bg),
            pl.BlockSpec((1, gps, T, Dh), bg),
            pl.BlockSpec((1, gps, V_AUG, T), bg),
            pl.BlockSpec(mT.shape, lambda b, g, i: (0, 0)),
        ],
        out_specs=pl.BlockSpec((1, gps, GROUP, Dh, tq), lambda b, g, i: (b, g, 0, 0, i)),
        out_shape=jax.ShapeDtypeStruct((B, G, GROUP, Dh, T), F32),
        scratch_shapes=[pltpu.VMEM((gps, T // kt_size * BIAS_ROWS, 2 * tq), BF16),
                        pltpu.VMEM((gps, n_sel, tq), F32)],
        compiler_params=_cparams(("arbitrary", "arbitrary", "arbitrary")),
        name="prompt_attn",
    )(qT, gT, kc, vcT, ksa, vsa, kw, vwa, mT)


def _dot_tn(v, p):
    return lax.dot_general(v, p, (((0,), (0,)), ((), ())), preferred_element_type=F32)


PAGES_PER_STEP = 8


def _page_specs(block_streams, stream_block, page):
    def spec(j):
        return pl.BlockSpec((1, block_streams, N_KV, HEAD_DIM, page),
                            lambda b, kt, pt: (pt[b, kt * PAGES_PER_STEP + j], stream_block, 0, 0, 0))
    return spec


def _sample_compress_kernel(pt_ref, *refs, page):
    del pt_ref
    pages = refs[:PAGES_PER_STEP]
    pe_ref, w1_ref, wp_ref, b1_ref, w2_ref, o_ref, panel, xs = refs[PAGES_PER_STEP:]
    kt = pl.program_id(1)
    cps = (page // CMP_STRIDE) * PAGES_PER_STEP
    row0 = pl.multiple_of(kt * cps, cps)
    for st in range(2):
        for gp in range(N_KV // 2):
            for j, p_ref in enumerate(pages):
                tile = p_ref[0, st, 2 * gp:2 * gp + 2].reshape(2 * HEAD_DIM, page)
                panel[j * page:(j + 1) * page, :] = tile.T
            for s in range(CMP_STRIDE):
                xs[st, gp, s, pl.ds(row0, cps), :] = panel[pl.ds(s, cps, stride=CMP_STRIDE), :].astype(BF16)

    @pl.when(kt == pl.num_programs(1) - 1)
    def _():
        for st in range(2):
            base = _dot(pe_ref[st], w1_ref[st])[0:1, :] + b1_ref[st]
            for gp in range(N_KV // 2):
                acc = None
                for sp in range(CMP_STRIDE // 2):
                    lhs = jnp.concatenate([xs[st, gp, 2 * sp], xs[st, gp, 2 * sp + 1]], axis=1)
                    d = _dot(lhs, wp_ref[st, sp])
                    acc = d if acc is None else acc + d
                for g2 in range(2):
                    parts = acc[:, g2 * 2 * CMP_HID:(g2 + 1) * 2 * CMP_HID]
                    o_ref[st, 0, 2 * gp + g2] = _compress_finish(parts, base, w2_ref[st])


def _sample_compress(cache_t, page_table, pe, w1, wp, b1, w2):
    page = cache_t.shape[-1]
    DB, n_pages = page_table.shape
    nchunk = n_pages * page // CMP_STRIDE
    assert n_pages % PAGES_PER_STEP == 0 and 2 * HEAD_DIM == 128 and page == 128
    const = lambda a: pl.BlockSpec(a.shape, lambda b, kt, pt: (0,) * a.ndim)
    spec = _page_specs(2, 0, page)
    return pl.pallas_call(
        functools.partial(_sample_compress_kernel, page=page),
        grid_spec=pltpu.PrefetchScalarGridSpec(
            num_scalar_prefetch=1,
            grid=(DB, n_pages // PAGES_PER_STEP),
            in_specs=[spec(j) for j in range(PAGES_PER_STEP)] + [const(pe), const(w1), const(wp), const(b1), const(w2)],
            out_specs=pl.BlockSpec((2, 1, N_KV, nchunk, HEAD_DIM), lambda b, kt, pt: (0, b, 0, 0, 0)),
            scratch_shapes=[pltpu.VMEM((PAGES_PER_STEP * page, 2 * HEAD_DIM), F32),
                            pltpu.VMEM((2, N_KV // 2, CMP_STRIDE, nchunk, 2 * HEAD_DIM), BF16)],
        ),
        out_shape=jax.ShapeDtypeStruct((2, DB, N_KV, nchunk, HEAD_DIM), BF16),
        compiler_params=_cparams(("arbitrary", "arbitrary")),
        name="sample_compress",
    )(page_table, *([cache_t] * PAGES_PER_STEP), pe, w1, wp, b1, w2)


def _sample_attn_kernel(pt_ref, *refs, t_new, past, page):
    del pt_ref
    pages = refs[:PAGES_PER_STEP]
    (q_ref, g_ref, kc_ref, vc_ref, kn_ref, vn_ref, kw_ref, vw_ref, kwn_ref, vwn_ref, mt_ref, rr_ref,
     o_ref, bias_ref, m_ref, l_ref, acc_ref, oc_ref) = refs[PAGES_PER_STEP:]
    kt = pl.program_id(1)
    gd = N_KV * HEAD_DIM
    R = N_KV * GROUP * t_new
    qbd = q_ref[0]
    lane = lax.broadcasted_iota(jnp.int32, (1, R), 1)
    i_lane = lane % t_new
    t_lane = past + i_lane

    @pl.when(kt == 0)
    def _():
        n_cmp = kc_ref.shape[2]
        sc = None
        for g in range(N_KV):
            d = _dot(kc_ref[0, g], qbd[g * HEAD_DIM:(g + 1) * HEAD_DIM, :])
            sc = d if sc is None else sc + d
        n_idx = lax.broadcasted_iota(jnp.int32, (n_cmp, R), 0)
        p_cmp = _softmax_cols(sc, n_idx * CMP_STRIDE + (CMP_BLOCK - 1) <= t_lane)
        pb = p_cmp.astype(BF16)
        o_cmp = jnp.concatenate([_dot_tn(vc_ref[0, g], pb) for g in range(N_KV)], axis=0)

        hi = lax.Precision.HIGHEST
        imp = jnp.dot(p_cmp, rr_ref[...], precision=hi, preferred_element_type=F32)
        selimp = jnp.dot(mt_ref[...], imp, precision=hi, preferred_element_type=F32)
        sel = _select_blocks(selimp, t_lane // SEL_BLOCK, N_SELECT)
        bias_ref[...] = jnp.where(sel, 0.0, NEG)

        wbuf = kw_ref.shape[-1]
        sw = jnp.concatenate([_dot_tn(kw_ref[0].astype(BF16), qbd), _dot(kwn_ref[0], qbd)], axis=0)
        jw = lax.broadcasted_iota(jnp.int32, (wbuf + t_new, R), 0)
        dpos = wbuf + i_lane - jw
        p_win = _softmax_cols(sw, (dpos >= 0) & (dpos < WINDOW) & (past - wbuf + jw >= 0)).astype(BF16)
        o_win = _dot(vw_ref[0].astype(BF16), p_win[0:wbuf]) + _dot_tn(vwn_ref[0], p_win[wbuf:wbuf + t_new])
        oc_ref[...] = g_ref[0, 0:1, :] * o_cmp + g_ref[0, 2:3, :] * o_win

        new_blk = past // SEL_BLOCK
        s_new = _dot(kn_ref[0], qbd) + bias_ref[new_blk:new_blk + 1, :]
        ip = lax.broadcasted_iota(jnp.int32, (t_new, R), 0)
        s_new = jnp.where(ip <= i_lane, s_new, NEG)
        m0 = jnp.max(s_new, axis=0, keepdims=True)
        p0 = jnp.exp2(s_new - m0)
        m_ref[...] = m0
        l_ref[...] = jnp.sum(p0, axis=0, keepdims=True)
        acc_ref[...] = _dot_tn(vn_ref[0], p0.astype(BF16))

    blk_per_page = page // SEL_BLOCK
    nb = PAGES_PER_STEP * blk_per_page
    b_t = bias_ref[pl.ds(pl.multiple_of(kt * nb, nb), nb), :]
    scores = []
    for j, p_ref in enumerate(pages):
        s = _dot_tn(p_ref[0, 0].reshape(gd, page).astype(BF16), qbd)
        scores.append(jnp.concatenate(
            [s[jb * SEL_BLOCK:(jb + 1) * SEL_BLOCK] + b_t[j * blk_per_page + jb:j * blk_per_page + jb + 1, :]
             for jb in range(blk_per_page)], axis=0))
    m_old = m_ref[...]
    m_new = m_old
    for s in scores:
        m_new = jnp.maximum(m_new, jnp.max(s, axis=0, keepdims=True))
    alpha = jnp.exp2(m_old - m_new)
    l = alpha * l_ref[...]
    acc = alpha * acc_ref[...]
    for s, p_ref in zip(scores, pages):
        p = jnp.exp2(s - m_new)
        l = l + jnp.sum(p, axis=0, keepdims=True)
        acc = acc + _dot(p_ref[0, 1].reshape(gd, page).astype(BF16), p.astype(BF16))
    m_ref[...] = m_new
    l_ref[...] = l
    acc_ref[...] = acc

    @pl.when(kt == pl.num_programs(1) - 1)
    def _():
        o = oc_ref[...] + g_ref[0, 1:2, :] * (acc / jnp.maximum(l, 1e-30))
        grp = lane // (GROUP * t_new)
        out = jnp.zeros((HEAD_DIM, R), F32)
        for g in range(N_KV):
            out = out + jnp.where(grp == g, o[g * HEAD_DIM:(g + 1) * HEAD_DIM, :], 0.0)
        o_ref[0] = out


def _sample_attn(cache_t, page_table, qbd, gT, kcv, kn, vn, kw, vw, kwn, vwn, mT, rr, *, t_new):
    page = cache_t.shape[-1]
    DB, n_pages = page_table.shape
    past = n_pages * page
    R = qbd.shape[-1]
    per_b = lambda a: pl.BlockSpec((1,) + a.shape[1:], lambda b, kt, pt: (b,) + (0,) * (a.ndim - 1))
    const = lambda a: pl.BlockSpec(a.shape, lambda b, kt, pt: (0,) * a.ndim)
    cmp_spec = lambda st: pl.BlockSpec((None, 1) + kcv.shape[2:], lambda b, kt, pt: (st, b, 0, 0, 0))
    spec = _page_specs(2, 1, page)
    return pl.pallas_call(
        functools.partial(_sample_attn_kernel, t_new=t_new, past=past, page=page),
        grid_spec=pltpu.PrefetchScalarGridSpec(
            num_scalar_prefetch=1,
            grid=(DB, n_pages // PAGES_PER_STEP),
            in_specs=[spec(j) for j in range(PAGES_PER_STEP)]
            + [per_b(qbd), per_b(gT), cmp_spec(0), cmp_spec(1), per_b(kn), per_b(vn), per_b(kw), per_b(vw),
               per_b(kwn), per_b(vwn), const(mT), const(rr)],
            out_specs=pl.BlockSpec((1, HEAD_DIM, R), lambda b, kt, pt: (b, 0, 0)),
            scratch_shapes=[pltpu.VMEM((mT.shape[0], R), F32), pltpu.VMEM((1, R), F32), pltpu.VMEM((1, R), F32),
                            pltpu.VMEM((N_KV * HEAD_DIM, R), F32), pltpu.VMEM((N_KV * HEAD_DIM, R), F32)],
        ),
        out_shape=jax.ShapeDtypeStruct((DB, HEAD_DIM, R), F32),
        compiler_params=_cparams(("arbitrary", "arbitrary")),
        name="sample_attn",
    )(page_table, *([cache_t] * PAGES_PER_STEP), qbd, gT, kcv, kcv, kn, vn, kw, vw, kwn, vwn, mT, rr)


def _nsa_outproj_kernel(o_ref, sz_ref, x_ref, w_ref, g_ref, b_ref, y_ref):
    gated = (o_ref[...] * sz_ref[...]).astype(BF16)
    y = _dot(gated, w_ref[...])
    y_ref[...] = _layer_norm(ALPHA * x_ref[...] + y, g_ref[...], b_ref[...])


def _nsa_outproj(o, sz, x, w, g, b, *, tm):
    N, D = x.shape
    row = lambda i: (i, 0)
    const = lambda i: (0, 0)
    return pl.pallas_call(
        _nsa_outproj_kernel,
        grid=(N // tm,),
        in_specs=[pl.BlockSpec((tm, o.shape[1]), row), pl.BlockSpec((tm, sz.shape[1]), row),
                  pl.BlockSpec((tm, D), row), pl.BlockSpec(w.shape, const),
                  pl.BlockSpec(g.shape, const), pl.BlockSpec(b.shape, const)],
        out_specs=pl.BlockSpec((tm, D), row),
        out_shape=jax.ShapeDtypeStruct((N, D), F32),
        compiler_params=_cparams(("arbitrary",)),
        name="nsa_outproj",
    )(o, sz, x, w, g, b)


def _prep_nsa_weights(w_in, pe, w1, b1, w2):
    D = w_in.shape[0]
    att_w = N_KV * GROUP * HEAD_DIM
    kv_w = 6 * N_KV * HEAD_DIM
    o1, o2 = att_w, att_w + kv_w
    o3 = o2 + 3 * N_KV * GROUP
    w_q = w_in[:, :o1] * (HEAD_DIM ** -0.5 * np.log2(np.e))
    w_g = w_in[:, o2:o3].reshape(D, N_KV, GROUP, 3).transpose(0, 3, 1, 2).reshape(D, 3 * N_KV * GROUP)
    w_g = jnp.pad(w_g, ((0, 0), (0, 128 - w_g.shape[1])))
    w_all = jnp.concatenate([w_q, w_in[:, o1:o2], w_in[:, o3:], w_g], axis=1).astype(BF16)
    r = CMP_BLOCK // CMP_STRIDE
    pe_rows = jnp.pad(pe.reshape(2, 1, CMP_BLOCK * HEAD_DIM), ((0, 0), (0, 7), (0, 0))).astype(BF16)
    w1_flat = w1.reshape(2, CMP_BLOCK * HEAD_DIM, CMP_HID).astype(BF16)
    w1r = w1.reshape(2, r, CMP_STRIDE * HEAD_DIM, CMP_HID)
    w1_split = jnp.concatenate([w1r[:, j] for j in range(r)], axis=-1).astype(BF16)
    w1p = w1.reshape(2, r, CMP_STRIDE // 2, 2, HEAD_DIM, CMP_HID).transpose(0, 2, 3, 4, 1, 5)
    w1p = w1p.reshape(2, CMP_STRIDE // 2, 2, HEAD_DIM, r * CMP_HID)
    eye = jnp.eye(2, dtype=w1.dtype)
    w1_pair = (w1p[:, :, :, None, :, None, :] * eye[None, None, None, :, None, :, None])
    w1_pair = w1_pair.reshape(2, CMP_STRIDE // 2, 4 * HEAD_DIM, 2 * r * CMP_HID).astype(BF16)
    return w_all, pe_rows, w1_flat, w1_split, w1_pair, b1.reshape(2, 1, CMP_HID), w2.astype(BF16)


def _nsa_prompt(x1, nsa_w, w_out, g, b):
    B, T, D = x1.shape
    w_all, pe_rows, w1_flat, w1_split, w1_pair, b1, w2 = nsa_w
    att_w = N_KV * GROUP * HEAD_DIM
    kv_w = 6 * N_KV * HEAD_DIM
    x_flat = x1.reshape(B * T, D)
    q, kv, sz, gates = _nsa_inproj(x_flat, w_all, d_q=att_w, d_kv=kv_w, tm=512)
    kv6 = kv.reshape(B, T, 6, N_KV, HEAD_DIM)
    rows = kv6[:, :, :4]
    win = kv6[:, T - min(WINDOW, T):, 4:]
    qT = q.reshape(B, T, N_KV, GROUP, HEAD_DIM).transpose(0, 2, 3, 4, 1)
    gT = gates[:, :3 * N_KV * GROUP].reshape(B, T, 3, N_KV, GROUP).transpose(0, 3, 2, 4, 1)
    gT = jnp.pad(gT.reshape(B, N_KV, 3 * GROUP, T), ((0, 0), (0, 0), (0, 16 - 3 * GROUP), (0, 0)))
    kvb = kv6.astype(BF16).transpose(2, 0, 3, 1, 4)
    nchunk = T // CMP_STRIDE
    x_cmp = kvb[0:2].reshape(2, B * N_KV, nchunk, CMP_STRIDE * HEAD_DIM)
    kcv = _compress(x_cmp, pe_rows, w1_flat, w1_split, b1, w2).reshape(2, B, N_KV, nchunk, HEAD_DIM)
    mT = jnp.asarray(_cmp_to_sel_matrix(T // SEL_BLOCK, nchunk))
    tr = lambda a: a.transpose(0, 1, 3, 2)
    kt_size = 256
    pos = np.arange(T)
    onehot = ((pos[:, None] % kt_size) // SEL_BLOCK == np.arange(K_AUG - HEAD_DIM)[None, :]).astype(np.float32)
    ksa = jnp.concatenate([kvb[2], jnp.broadcast_to(jnp.asarray(onehot, BF16), (B, N_KV, T, K_AUG - HEAD_DIM))], axis=-1)
    ones_rows = np.zeros((V_AUG - HEAD_DIM, T), np.float32)
    ones_rows[0] = 1.0
    ones_rows = jnp.broadcast_to(jnp.asarray(ones_rows, BF16), (B, N_KV, V_AUG - HEAD_DIM, T))
    aug = lambda v: jnp.concatenate([tr(v), ones_rows], axis=2)
    oT = _prompt_attn(qT, gT, kcv[0], tr(kcv[1]), ksa, aug(kvb[3]), kvb[4], aug(kvb[5]), mT, tq=128, kt_size=kt_size,
                      gps=PROMPT_GROUPS_PER_STEP)
    o = oT.transpose(0, 4, 1, 2, 3).reshape(B * T, att_w)
    y = _nsa_outproj(o, sz, x_flat, w_out, g, b, tm=512)
    return y.reshape(B, T, D), rows, win


def _nsa_sample(x1, cache_kv_l, cache_win_l, page_table, nsa_w, w_out, g, b):
    DB, T, D = x1.shape
    w_all, pe_rows, w1_flat, w1_split, w1_pair, b1, w2 = nsa_w
    att_w = N_KV * GROUP * HEAD_DIM
    kv_w = 6 * N_KV * HEAD_DIM
    page = cache_kv_l.shape[1]
    past = page_table.shape[1] * page
    assert T <= CMP_STRIDE and past % SEL_BLOCK == 0 and past % CMP_STRIDE == 0
    x_flat = x1.reshape(DB * T, D)
    q, kv, sz, gates = _nsa_inproj(x_flat, w_all, d_q=att_w, d_kv=kv_w, tm=DB * T)
    kv6 = kv.reshape(DB, T, 6, N_KV, HEAD_DIM)
    rows = kv6[:, :, :4]
    wseq = jnp.concatenate([cache_win_l, kv6[:, :, 4:]], axis=1)
    gd = N_KV * HEAD_DIM
    R = N_KV * GROUP * T
    qT = q.reshape(DB, T, N_KV, GROUP, HEAD_DIM).transpose(0, 2, 4, 3, 1)
    eye = jnp.eye(N_KV, dtype=BF16)
    qbd = (qT[:, :, :, None] * eye[None, :, None, :, None, None]).reshape(DB, gd, R)
    gT = gates[:, :3 * N_KV * GROUP].reshape(DB, T, 3, N_KV * GROUP).transpose(0, 2, 3, 1).reshape(DB, 3, R)
    gT = jnp.pad(gT, ((0, 0), (0, 5), (0, 0)))
    new_rows = lambda st: kv6[:, :, st].reshape(DB, T, gd).astype(BF16)
    cache_t = cache_kv_l.transpose(0, 2, 3, 4, 1)
    win_t = cache_win_l.transpose(0, 2, 3, 4, 1)
    wbuf = win_t.shape[-1]
    nchunk = past // CMP_STRIDE
    kcv = _sample_compress(cache_t, page_table, pe_rows, w1_flat, w1_pair, b1, w2)
    n_sel = past // SEL_BLOCK + 1
    n_sel_pad = -(-n_sel // (PAGES_PER_STEP * page // SEL_BLOCK)) * (PAGES_PER_STEP * page // SEL_BLOCK)
    mT = np.zeros((n_sel_pad, nchunk), np.float32)
    mT[:n_sel] = _cmp_to_sel_matrix(n_sel, nchunk)
    lane = np.arange(R)
    rr = ((lane[:, None] % T == lane[None, :] % T)
          & (lane[:, None] // (GROUP * T) == lane[None, :] // (GROUP * T))).astype(np.float32)
    oT = _sample_attn(cache_t, page_table, qbd, gT, kcv, new_rows(2), new_rows(3),
                      win_t[:, 0].reshape(DB, gd, wbuf), win_t[:, 1].reshape(DB, gd, wbuf), new_rows(4), new_rows(5),
                      jnp.asarray(mT), jnp.asarray(rr), t_new=T)
    o = oT.reshape(DB, HEAD_DIM, N_KV, GROUP, T).transpose(0, 4, 2, 3, 1).reshape(DB * T, att_w)
    y = _nsa_outproj(o, sz, x_flat, w_out, g, b, tm=DB * T)
    return y.reshape(DB, T, D), rows, wseq[:, T:]


def kernel(x_prompt, x_sample, state_conv, cache_kv, cache_win, page_table, conv_w_in, conv_b_in, conv_dw_w,
           conv_dw_b, conv_ln_g, conv_ln_b, conv_w_out, nsa_w_in, cmp_pe, cmp_w1, cmp_b1, cmp_w2, nsa_w_out,
           ln_g, ln_b):
    B, T, D = x_prompt.shape
    DB, TS, _ = x_sample.shape
    d_in = conv_w_out.shape[1]
    row = lambda v: v.reshape(1, -1)

    cw = (conv_w_in[0].astype(BF16), row(conv_b_in[0]), conv_dw_w[0], row(conv_dw_b[0]), row(conv_ln_g[0]),
          row(conv_ln_b[0]), conv_w_out[0].astype(BF16), row(ln_g[0]), row(ln_b[0]))
    pad = HALO - (CONV_W - 1)
    xp, sp = _conv_layer(x_prompt, jnp.zeros((B, HALO, d_in), F32), *cw, ts=256, carry=True)
    st = jnp.pad(state_conv[0], ((0, 0), (pad, 0), (0, 0)))
    xs, ss = _conv_layer(x_sample, st, *cw, ts=TS, carry=False)

    nsa_w = _prep_nsa_weights(nsa_w_in[0], cmp_pe[0], cmp_w1[0], cmp_b1[0], cmp_w2[0])
    w_out = nsa_w_out[0].astype(BF16)
    yp, rp, wp = _nsa_prompt(xp, nsa_w, w_out, row(ln_g[1]), row(ln_b[1]))
    ys, rs, ws = _nsa_sample(xs, cache_kv[0], cache_win[0], page_table, nsa_w, w_out, row(ln_g[1]), row(ln_b[1]))
    return (yp, ys, sp[None, :, pad:], ss[None, :, pad:], rp[None], rs[None], wp[None], ws[None])
```

```python
import functools

import jax
import jax.numpy as jnp
import numpy as np
from jax import lax
from jax.experimental import pallas as pl
from jax.experimental.pallas import tpu as pltpu

F32 = jnp.float32
BF16 = jnp.bfloat16

CONV_W = 31
N_KV = 4
GROUP = 4
HEAD_DIM = 64
CMP_BLOCK = 32
CMP_STRIDE = 16
CMP_HID = 128
SEL_BLOCK = 64
N_SELECT = 16
WINDOW = 512
DEPTH = 2
ALPHA = (2 * DEPTH) ** 0.25
LN_EPS = 1e-5
NEG = -1e30

VMEM_LIMIT = 56 * 1024 * 1024
LANES = 128
HALO = 32


def _cparams(sem):
    return pltpu.CompilerParams(dimension_semantics=sem, vmem_limit_bytes=VMEM_LIMIT)


def _sigmoid(x):
    return 1.0 / (1.0 + jnp.exp(-x))


def _layer_norm(x, g, b):
    mu = jnp.mean(x, axis=-1, keepdims=True)
    xc = x - mu
    var = jnp.mean(xc * xc, axis=-1, keepdims=True)
    return xc * lax.rsqrt(var + LN_EPS) * g + b


def _dot(a, b):
    return jnp.dot(a, b, preferred_element_type=F32)


def _conv_layer_kernel(x_ref, st_ref, win_ref, bin_ref, dww_ref, dwb_ref, clg_ref, clb_ref, wout_ref,
                       lng_ref, lnb_ref, y_ref, ns_ref, ubuf, zbuf, cbuf, *, ns, ts, d_in, carry):
    t = pl.program_id(1)
    rows = ns * ts
    cb = LANES
    n_cb = d_in // cb
    lanes = lambda ci: slice(ci * cb, (ci + 1) * cb)

    if carry:
        @pl.when(t == 0)
        def _():
            ubuf[:, :, 0:HALO, :] = jnp.zeros((ns, n_cb, HALO, cb), F32)
    else:
        for ci in range(n_cb):
            ubuf[:, ci, 0:HALO, :] = st_ref[:, :, lanes(ci)]

    x = x_ref[...].reshape(rows, x_ref.shape[-1])
    xb = x.astype(BF16)
    a = _dot(xb, win_ref[:, 0:d_in]) + bin_ref[:, 0:d_in]
    gl = _dot(xb, win_ref[:, d_in:2 * d_in]) + bin_ref[:, d_in:2 * d_in]
    u = (a * _sigmoid(gl)).reshape(ns, ts, d_in)
    for ci in range(n_cb):
        ubuf[:, ci, HALO:HALO + ts, :] = u[:, :, lanes(ci)]
    z = _dot(xb, win_ref[:, 2 * d_in:3 * d_in]) + bin_ref[:, 2 * d_in:3 * d_in]
    zbuf[...] = z * _sigmoid(z)

    rb = min(ts, 64)
    n_rb = ts // rb
    off0 = HALO - (CONV_W - 1)

    def chunk(i, c):
        s = i // (n_rb * n_cb)
        rem = i % (n_rb * n_cb)
        r0 = pl.multiple_of((rem // n_cb) * rb, rb)
        ci = rem % n_cb
        c0 = pl.multiple_of(ci * cb, cb)
        acc = jnp.zeros((rb, cb), F32) + dwb_ref[:, pl.ds(c0, cb)]
        for k in range(CONV_W):
            acc = acc + dww_ref[k:k + 1, pl.ds(c0, cb)] * ubuf[s, ci, pl.ds(r0 + off0 + k, rb, stride=1), :]
        cbuf[s, pl.ds(r0, rb), pl.ds(c0, cb)] = acc
        return c

    lax.fori_loop(0, ns * n_rb * n_cb, chunk, 0)

    for ci in range(n_cb):
        ns_ref[:, :, lanes(ci)] = ubuf[:, ci, ts:ts + HALO, :]
    if carry:
        ubuf[:, :, 0:HALO, :] = ubuf[:, :, ts:ts + HALO, :]

    yc = _layer_norm(cbuf[...].reshape(rows, d_in), clg_ref[...], clb_ref[...])
    gated = (yc * _sigmoid(yc)) * zbuf[...]
    out = _dot(gated.astype(BF16), wout_ref[...])
    y = _layer_norm(ALPHA * x + out, lng_ref[...], lnb_ref[...])
    y_ref[...] = y.reshape(y_ref.shape)


def _conv_layer(x, state, w_in, b_in, dw_w, dw_b, cl_g, cl_b, w_out, ln_g, ln_b, *, ts, carry):
    S, T, D = x.shape
    d_in = w_out.shape[0]
    ns = 1 if carry else S
    grid = (S // ns, T // ts)
    const = lambda s, t: (0, 0)
    kernel = functools.partial(_conv_layer_kernel, ns=ns, ts=ts, d_in=d_in, carry=carry)
    y, new_state = pl.pallas_call(
        kernel,
        grid=grid,
        in_specs=[
            pl.BlockSpec((ns, ts, D), lambda s, t: (s, t, 0)),
            pl.BlockSpec((ns, HALO, d_in), lambda s, t: (s, 0, 0)),
            pl.BlockSpec(w_in.shape, const),
            pl.BlockSpec(b_in.shape, const),
            pl.BlockSpec(dw_w.shape, const),
            pl.BlockSpec(dw_b.shape, const),
            pl.BlockSpec(cl_g.shape, const),
            pl.BlockSpec(cl_b.shape, const),
            pl.BlockSpec(w_out.shape, const),
            pl.BlockSpec(ln_g.shape, const),
            pl.BlockSpec(ln_b.shape, const),
        ],
        out_specs=[
            pl.BlockSpec((ns, ts, D), lambda s, t: (s, t, 0)),
            pl.BlockSpec((ns, HALO, d_in), lambda s, t: (s, 0, 0)),
        ],
        out_shape=[
            jax.ShapeDtypeStruct((S, T, D), F32),
            jax.ShapeDtypeStruct((S, HALO, d_in), F32),
        ],
        scratch_shapes=[
            pltpu.VMEM((ns, d_in // LANES, HALO + ts, LANES), F32),
            pltpu.VMEM((ns * ts, d_in), F32),
            pltpu.VMEM((ns, ts, d_in), F32),
        ],
        compiler_params=_cparams(("arbitrary", "arbitrary")),
        name="conv_layer_carry" if carry else "conv_layer_state",
    )(x, state, w_in, b_in, dw_w, dw_b, cl_g, cl_b, w_out, ln_g, ln_b)
    return y, new_state


def _nsa_inproj_kernel(x_ref, w_ref, q_ref, kv_ref, sz_ref, g_ref, *, d_q, d_kv):
    xb = x_ref[...].astype(BF16)
    q_ref[...] = _dot(xb, w_ref[:, 0:d_q]).astype(BF16)
    kv_ref[...] = _dot(xb, w_ref[:, d_q:d_q + d_kv])
    z = _dot(xb, w_ref[:, d_q + d_kv:2 * d_q + d_kv])
    sz_ref[...] = z * _sigmoid(z)
    g_ref[...] = _sigmoid(_dot(xb, w_ref[:, 2 * d_q + d_kv:]))


def _nsa_inproj(x, w, *, d_q, d_kv, tm):
    N, D = x.shape
    d_g = w.shape[1] - 2 * d_q - d_kv
    row = lambda i: (i, 0)
    return pl.pallas_call(
        functools.partial(_nsa_inproj_kernel, d_q=d_q, d_kv=d_kv),
        grid=(N // tm,),
        in_specs=[pl.BlockSpec((tm, D), row), pl.BlockSpec(w.shape, lambda i: (0, 0))],
        out_specs=[pl.BlockSpec((tm, d_q), row), pl.BlockSpec((tm, d_kv), row),
                   pl.BlockSpec((tm, d_q), row), pl.BlockSpec((tm, d_g), row)],
        out_shape=[jax.ShapeDtypeStruct((N, d_q), BF16), jax.ShapeDtypeStruct((N, d_kv), F32),
                   jax.ShapeDtypeStruct((N, d_q), F32), jax.ShapeDtypeStruct((N, d_g), F32)],
        compiler_params=_cparams(("arbitrary",)),
        name="nsa_inproj",
    )(x, w)


def _nsa_inproj_prompt_kernel(x_ref, wn_ref, wt_ref, q_ref, sz_ref, g_ref, ksa_ref, kwa_ref, kvt_ref, va_ref,
                              *, d_q, kt_size):
    tm = x_ref.shape[1]
    t0 = pl.program_id(1) * tm
    xb = x_ref[0].astype(BF16)
    q_ref[0] = _dot(xb, wn_ref[:, 0:d_q]).astype(BF16)
    z = _dot(xb, wn_ref[:, d_q:2 * d_q])
    sz_ref[0] = z * _sigmoid(z)
    o = 2 * d_q
    g_ref[0] = _sigmoid(_dot(xb, wn_ref[:, o:o + LANES]))
    o += LANES
    kw = N_KV * K_AUG
    pos = t0 + lax.broadcasted_iota(jnp.int32, (tm, kw), 0)
    lane = lax.broadcasted_iota(jnp.int32, (tm, kw), 1) % K_AUG
    onehot = jnp.where(lane - HEAD_DIM == (pos % kt_size) // SEL_BLOCK, 1.0, 0.0)
    ksa_ref[0] = (_dot(xb, wn_ref[:, o:o + kw]) + onehot).astype(BF16)
    kwa_ref[0] = _dot(xb, wn_ref[:, o + kw:o + 2 * kw]).astype(BF16)
    hT = lax.dot_general(wt_ref[...], xb, (((1,), (1,)), ((), ())), preferred_element_type=F32)
    kvt_ref[0] = hT
    ones = jnp.concatenate([jnp.ones((1, tm), F32), jnp.zeros((V_AUG - HEAD_DIM - 1, tm), F32)], axis=0).astype(BF16)
    for i, st in enumerate((3, 5)):
        for g in range(N_KV):
            r0 = (st * N_KV + g) * HEAD_DIM
            va_ref[0, i, g, 0:HEAD_DIM, :] = hT[r0:r0 + HEAD_DIM].astype(BF16)
            va_ref[0, i, g, HEAD_DIM:V_AUG, :] = ones


def _nsa_inproj_prompt(x, w_nat, w_kvt, *, d_q, kt_size, tm):
    B, T, D = x.shape
    kw = N_KV * K_AUG
    tok = lambda n: pl.BlockSpec((1, tm, n), lambda b, t: (b, t, 0))
    const = lambda a: pl.BlockSpec(a.shape, lambda b, t: (0, 0))
    return pl.pallas_call(
        functools.partial(_nsa_inproj_prompt_kernel, d_q=d_q, kt_size=kt_size),
        grid=(B, T // tm),
        in_specs=[tok(D), const(w_nat), const(w_kvt)],
        out_specs=[tok(d_q), tok(d_q), tok(LANES), tok(kw), tok(kw),
                   pl.BlockSpec((1, w_kvt.shape[0], tm), lambda b, t: (b, 0, t)),
                   pl.BlockSpec((1, 2, N_KV, V_AUG, tm), lambda b, t: (b, 0, 0, 0, t))],
        out_shape=[jax.ShapeDtypeStruct((B, T, d_q), BF16), jax.ShapeDtypeStruct((B, T, d_q), F32),
                   jax.ShapeDtypeStruct((B, T, LANES), F32), jax.ShapeDtypeStruct((B, T, kw), BF16),
                   jax.ShapeDtypeStruct((B, T, kw), BF16), jax.ShapeDtypeStruct((B, w_kvt.shape[0], T), F32),
                   jax.ShapeDtypeStruct((B, 2, N_KV, V_AUG, T), BF16)],
        compiler_params=_cparams(("arbitrary", "arbitrary")),
        name="nsa_inproj_prompt",
    )(x, w_nat, w_kvt)


def _compress_finish(parts, base, w2):
    nchunk = parts.shape[0]
    second = pltpu.roll(parts[:, CMP_HID:], nchunk - 1, 0)
    row = lax.broadcasted_iota(jnp.int32, (nchunk, CMP_HID), 0)
    second = jnp.where(row < nchunk - 1, second, 0.0)
    h = base + parts[:, :CMP_HID] + second
    act = h * _sigmoid(h)
    return _dot(act.astype(BF16), w2).astype(BF16)


def _dot_f32_by_const(x, c, *, const_first):
    hi = x.astype(BF16)
    r1 = x - hi.astype(F32)
    mid = r1.astype(BF16)
    lo = (r1 - mid.astype(F32)).astype(BF16)
    out = None
    for part in (hi, mid, lo):
        d = _dot(c, part) if const_first else _dot(part, c)
        out = d if out is None else out + d
    return out


def _softmax_cols(s, valid):
    s = jnp.where(valid, s, NEG)
    m = jnp.max(s, axis=0, keepdims=True)
    e = jnp.where(valid, jnp.exp2(s - m), 0.0)
    return e / jnp.maximum(jnp.sum(e, axis=0, keepdims=True), 1e-30)


def _selection_scores(selimp, cur):
    j = lax.broadcasted_iota(jnp.int32, selimp.shape, 0)
    causal = j <= cur
    forced = (j == 0) | (j == cur) | (j == cur - 1)
    return jnp.where(forced, jnp.inf, jnp.where(causal, selimp, -jnp.inf)), causal


def _select_blocks(selimp, cur, n_select):
    ns = selimp.shape[0]
    j = lax.broadcasted_iota(jnp.int32, selimp.shape, 0)
    score, causal = _selection_scores(selimp, cur)
    taken = jnp.zeros(selimp.shape, jnp.int32)
    for _ in range(n_select):
        m = jnp.max(score, axis=0, keepdims=True)
        first = jnp.min(jnp.where(score == m, j, ns), axis=0, keepdims=True)
        hit = j == first
        taken = jnp.where(hit, 1, taken)
        score = jnp.where(hit, -jnp.inf, score)
    return (taken > 0) & causal


def _select_blocks_by_rank(selimp, cur, n_select, score_ref):
    ns = selimp.shape[0]
    score, causal = _selection_scores(selimp, cur)
    score_ref[...] = score
    sub = lax.broadcasted_iota(jnp.int32, (8, selimp.shape[1]), 0)
    ranks = []
    for v in range(ns // 8):
        blk = score[8 * v:8 * v + 8]
        rank = jnp.zeros(blk.shape, F32)
        for jp in range(ns):
            row = score_ref[jp:jp + 1, :]
            if jp < 8 * v:
                ahead = row >= blk
            elif jp >= 8 * v + 8:
                ahead = row > blk
            else:
                ahead = (row > blk) | ((row == blk) & (sub > jp - 8 * v))
            rank = rank + jnp.where(ahead, 1.0, 0.0)
        ranks.append(rank)
    return (jnp.concatenate(ranks, axis=0) < n_select) & causal


def _cmp_to_sel_matrix(n_sel, n_cmp):
    r = SEL_BLOCK // CMP_STRIDE
    lo = -(CMP_BLOCK // CMP_STRIDE - 1)
    m = np.zeros((n_sel, n_cmp), np.float32)
    for o in range(lo, r):
        start = o * CMP_STRIDE
        ov = max(0, min(SEL_BLOCK, start + CMP_BLOCK) - max(0, start))
        for jb in range(n_sel):
            n = r * jb + o
            if ov > 0 and 0 <= n < n_cmp:
                m[jb, n] += ov / CMP_BLOCK
    return m


V_AUG = 80
K_AUG = 128
BIAS_ROWS = 16


def _prompt_attn_kernel(q_ref, g_ref, kc_ref, vct_ref, ksa_ref, vsa_ref, kwa_ref, vwa_ref, mt_ref, o_ref,
                        bias_ref, score_ref, *, tq, kt_size, gps):
    qi = pl.program_id(1)
    qs = qi * tq
    hp = GROUP // 2
    RH = 2 * tq
    n_cmp = kc_ref.shape[2]
    n_sel = mt_ref.shape[0]
    blk_per_tile = kt_size // SEL_BLOCK
    chains = [(gi, h) for gi in range(gps) for h in range(hp)]

    t1 = qs + lax.broadcasted_iota(jnp.int32, (1, tq), 1)
    t_lane = jnp.concatenate([t1, t1], axis=1)
    pair = lambda gi, h: slice((gi * hp + h) * 2 * HEAD_DIM, (gi * hp + h + 1) * 2 * HEAD_DIM)
    qT = {}
    for gi, h in chains:
        t2 = q_ref[0, :, pair(gi, h)].astype(F32).T.astype(BF16)
        qT[gi, h] = jnp.concatenate([t2[0:HEAD_DIM], t2[HEAD_DIM:2 * HEAD_DIM]], axis=1)
    gT = g_ref[0].T
    k_lanes = lambda gi: slice(gi * K_AUG, (gi + 1) * K_AUG)

    wk = WINDOW + tq
    start = pl.multiple_of(jnp.maximum(qs - WINDOW, 0), 128)
    s_cmp = {c: _dot(kc_ref[0, c[0]], qT[c]) for c in chains}
    k_pad = jnp.zeros((K_AUG - HEAD_DIM, RH), BF16)
    s_win = {c: _dot(kwa_ref[0, pl.ds(start, wk), k_lanes(c[0])], jnp.concatenate([qT[c], k_pad], axis=0))
             for c in chains}

    n_idx = lax.broadcasted_iota(jnp.int32, (n_cmp, RH), 0)
    cmp_valid = n_idx * CMP_STRIDE + (CMP_BLOCK - 1) <= t_lane
    o_cmp = {}
    imps = [None] * gps
    for gi, h in chains:
        p_cmp = _softmax_cols(s_cmp[gi, h], cmp_valid)
        o_cmp[gi, h] = _dot(vct_ref[0, gi], p_cmp.astype(BF16))
        part = p_cmp[:, 0:tq] + p_cmp[:, tq:2 * tq]
        imps[gi] = part if imps[gi] is None else imps[gi] + part

    dpos = t_lane - (start + lax.broadcasted_iota(jnp.int32, (wk, RH), 0))
    win_valid = (dpos >= 0) & (dpos < WINDOW)
    o_win = {}
    for gi, h in chains:
        sw = jnp.where(win_valid, s_win[gi, h], NEG)
        p_win = jnp.exp2(sw - jnp.max(sw, axis=0, keepdims=True)).astype(BF16)
        acc_win = _dot(vwa_ref[0, 0, gi, :, pl.ds(start, wk)], p_win)
        o_win[gi, h] = acc_win[0:HEAD_DIM] / jnp.maximum(acc_win[HEAD_DIM:HEAD_DIM + 1], 1e-30)

    pad = jnp.zeros((BIAS_ROWS - blk_per_tile, tq), F32)
    for gi in range(gps):
        selimp = _dot_f32_by_const(imps[gi], mt_ref[...], const_first=True)
        sel = _select_blocks_by_rank(selimp, t1 // SEL_BLOCK, N_SELECT, score_ref.at[gi])
        bias = jnp.where(sel, 0.0, NEG)
        rows = []
        for kt in range(n_sel // blk_per_tile):
            rows += [bias[kt * blk_per_tile:(kt + 1) * blk_per_tile], pad]
        bias_rows = jnp.concatenate(rows, axis=0)
        bias_ref[gi] = jnp.concatenate([bias_rows, bias_rows], axis=1).astype(BF16)

    q_pad = jnp.zeros((K_AUG - HEAD_DIM - BIAS_ROWS, RH), BF16)

    def sel_units(kt, carries, diagonal):
        k0 = pl.multiple_of(kt * kt_size, kt_size)
        scores = []
        for gi, h in chains:
            b_t = bias_ref[gi, pl.ds(pl.multiple_of(kt * BIAS_ROWS, BIAS_ROWS), BIAS_ROWS), :]
            qa = jnp.concatenate([qT[gi, h], b_t, q_pad], axis=0)
            scores.append(_dot(ksa_ref[0, pl.ds(k0, kt_size), k_lanes(gi)], qa))
        out = []
        for (gi, h), s, (m, acc) in zip(chains, scores, carries):
            if diagonal:
                kidx = k0 + lax.broadcasted_iota(jnp.int32, (kt_size, RH), 0)
                s = jnp.where(kidx <= t_lane, s, NEG)
            m_new = jnp.maximum(m, jnp.max(s, axis=0, keepdims=True))
            p = jnp.exp2(s - m_new).astype(BF16)
            acc = jnp.exp2(m - m_new) * acc + _dot(vsa_ref[0, 0, gi, :, pl.ds(k0, kt_size)], p)
            out.append((m_new, acc))
        return tuple(out)

    n_full = qs // kt_size
    init = tuple((jnp.full((1, RH), NEG, F32), jnp.zeros((V_AUG, RH), F32)) for _ in chains)
    carries = lax.fori_loop(0, n_full, lambda kt, c: sel_units(kt, c, False), init)
    carries = sel_units(n_full, carries, True)

    def gate(gi, h, branch):
        r0 = branch * N_KV * GROUP + gi * GROUP + 2 * h
        return jnp.concatenate([gT[r0:r0 + 1], gT[r0 + 1:r0 + 2]], axis=1)

    for i, (gi, h) in enumerate(chains):
        acc_sel = carries[i][1]
        o_sel = acc_sel[0:HEAD_DIM] / jnp.maximum(acc_sel[HEAD_DIM:HEAD_DIM + 1], 1e-30)
        o = gate(gi, h, 0) * o_cmp[gi, h] + gate(gi, h, 1) * o_sel + gate(gi, h, 2) * o_win[gi, h]
        o_ref[0, :, pair(gi, h)] = jnp.concatenate([o[:, 0:tq], o[:, tq:2 * tq]], axis=0).T


def _prompt_attn(q, gates, kc, vcT, ksa, kwa, va, mT, *, tq, kt_size):
    B, T, att_w = q.shape
    G, NC, Dh = kc.shape[1:]
    n_sel = mT.shape[0]
    assert kt_size % SEL_BLOCK == 0 and kt_size // SEL_BLOCK <= BIAS_ROWS and T % kt_size == 0 and kt_size % tq == 0
    assert tq == LANES and G == N_KV
    whole = lambda a: pl.BlockSpec((1,) + a.shape[1:], lambda b, i: (b,) + (0,) * (a.ndim - 1))
    v_spec = lambda s: pl.BlockSpec((1, 1, G, V_AUG, T), lambda b, i: (b, s, 0, 0, 0))
    return pl.pallas_call(
        functools.partial(_prompt_attn_kernel, tq=tq, kt_size=kt_size, gps=G),
        grid=(B, T // tq),
        in_specs=[
            pl.BlockSpec((1, tq, att_w), lambda b, i: (b, i, 0)),
            pl.BlockSpec((1, tq, gates.shape[2]), lambda b, i: (b, i, 0)),
            whole(kc), whole(vcT), whole(ksa), v_spec(0), whole(kwa), v_spec(1),
            pl.BlockSpec(mT.shape, lambda b, i: (0, 0)),
        ],
        out_specs=pl.BlockSpec((1, tq, att_w), lambda b, i: (b, i, 0)),
        out_shape=jax.ShapeDtypeStruct((B, T, att_w), F32),
        scratch_shapes=[pltpu.VMEM((G, T // kt_size * BIAS_ROWS, 2 * tq), BF16),
                        pltpu.VMEM((G, n_sel, tq), F32)],
        compiler_params=_cparams(("arbitrary", "arbitrary")),
        name="prompt_attn",
    )(q, gates, kc, vcT, ksa, va, kwa, va, mT)


def _dot_tn(v, p):
    return lax.dot_general(v, p, (((0,), (0,)), ((), ())), preferred_element_type=F32)


PAGES_PER_STEP = 8


def _page_specs(block_streams, stream_block, page):
    def spec(j):
        return pl.BlockSpec((1, block_streams, N_KV, HEAD_DIM, page),
                            lambda b, kt, pt: (pt[b, kt * PAGES_PER_STEP + j], stream_block, 0, 0, 0))
    return spec


def _sample_compress_kernel(pt_ref, *refs, page):
    del pt_ref
    pages = refs[:PAGES_PER_STEP]
    pe_ref, w1_ref, wp_ref, b1_ref, w2_ref, o_ref, panel, xs = refs[PAGES_PER_STEP:]
    kt = pl.program_id(1)
    cps = (page // CMP_STRIDE) * PAGES_PER_STEP
    row0 = pl.multiple_of(kt * cps, cps)
    for st in range(2):
        for gp in range(N_KV // 2):
            for j, p_ref in enumerate(pages):
                tile = p_ref[0, st, 2 * gp:2 * gp + 2].reshape(2 * HEAD_DIM, page)
                panel[j * page:(j + 1) * page, :] = tile.T
            for s in range(CMP_STRIDE):
                xs[st, gp, s, pl.ds(row0, cps), :] = panel[pl.ds(s, cps, stride=CMP_STRIDE), :].astype(BF16)

    @pl.when(kt == pl.num_programs(1) - 1)
    def _():
        for st in range(2):
            base = _dot(pe_ref[st], w1_ref[st])[0:1, :] + b1_ref[st]
            for gp in range(N_KV // 2):
                acc = None
                for sp in range(CMP_STRIDE // 2):
                    lhs = jnp.concatenate([xs[st, gp, 2 * sp], xs[st, gp, 2 * sp + 1]], axis=1)
                    d = _dot(lhs, wp_ref[st, sp])
                    acc = d if acc is None else acc + d
                for g2 in range(2):
                    parts = acc[:, g2 * 2 * CMP_HID:(g2 + 1) * 2 * CMP_HID]
                    o_ref[st, 0, 2 * gp + g2] = _compress_finish(parts, base, w2_ref[st])


def _sample_compress(cache_t, page_table, pe, w1, wp, b1, w2, *, name="sample_compress"):
    page = 2 * HEAD_DIM
    if page_table is None:
        DB, n_pages = cache_t.shape[0], cache_t.shape[-1] // page
        page_table = jnp.zeros((1, 1), jnp.int32)

        def spec(j):
            return pl.BlockSpec((1, 2, N_KV, HEAD_DIM, page), lambda b, kt, pt: (b, 0, 0, 0, kt * PAGES_PER_STEP + j))
    else:
        DB, n_pages = page_table.shape
        assert cache_t.shape[-1] == page
        spec = _page_specs(2, 0, page)
    nchunk = n_pages * page // CMP_STRIDE
    assert n_pages % PAGES_PER_STEP == 0 and page == LANES
    const = lambda a: pl.BlockSpec(a.shape, lambda b, kt, pt: (0,) * a.ndim)
    return pl.pallas_call(
        functools.partial(_sample_compress_kernel, page=page),
        grid_spec=pltpu.PrefetchScalarGridSpec(
            num_scalar_prefetch=1,
            grid=(DB, n_pages // PAGES_PER_STEP),
            in_specs=[spec(j) for j in range(PAGES_PER_STEP)] + [const(pe), const(w1), const(wp), const(b1), const(w2)],
            out_specs=pl.BlockSpec((2, 1, N_KV, nchunk, HEAD_DIM), lambda b, kt, pt: (0, b, 0, 0, 0)),
            scratch_shapes=[pltpu.VMEM((PAGES_PER_STEP * page, 2 * HEAD_DIM), F32),
                            pltpu.VMEM((2, N_KV // 2, CMP_STRIDE, nchunk, 2 * HEAD_DIM), BF16)],
        ),
        out_shape=jax.ShapeDtypeStruct((2, DB, N_KV, nchunk, HEAD_DIM), BF16),
        compiler_params=_cparams(("arbitrary", "arbitrary")),
        name=name,
    )(page_table, *([cache_t] * PAGES_PER_STEP), pe, w1, wp, b1, w2)


def _sample_attn_kernel(pt_ref, *refs, t_new, past, page):
    del pt_ref
    pages = refs[:PAGES_PER_STEP]
    (q_ref, g_ref, kc_ref, vc_ref, kn_ref, vn_ref, kw_ref, vw_ref, kwn_ref, vwn_ref, mt_ref, rr_ref,
     o_ref, bias_ref, m_ref, l_ref, acc_ref, oc_ref) = refs[PAGES_PER_STEP:]
    kt = pl.program_id(1)
    gd = N_KV * HEAD_DIM
    R = N_KV * GROUP * t_new
    qbd = q_ref[0]
    lane = lax.broadcasted_iota(jnp.int32, (1, R), 1)
    i_lane = lane % t_new
    t_lane = past + i_lane

    @pl.when(kt == 0)
    def _():
        n_cmp = kc_ref.shape[2]
        sc = None
        for g in range(N_KV):
            d = _dot(kc_ref[0, g], qbd[g * HEAD_DIM:(g + 1) * HEAD_DIM, :])
            sc = d if sc is None else sc + d
        n_idx = lax.broadcasted_iota(jnp.int32, (n_cmp, R), 0)
        p_cmp = _softmax_cols(sc, n_idx * CMP_STRIDE + (CMP_BLOCK - 1) <= t_lane)
        pb = p_cmp.astype(BF16)
        o_cmp = jnp.concatenate([_dot_tn(vc_ref[0, g], pb) for g in range(N_KV)], axis=0)

        imp = _dot_f32_by_const(p_cmp, rr_ref[...], const_first=False)
        selimp = _dot_f32_by_const(imp, mt_ref[...], const_first=True)
        sel = _select_blocks(selimp, t_lane // SEL_BLOCK, N_SELECT)
        bias_ref[...] = jnp.where(sel, 0.0, NEG)

        wbuf = kw_ref.shape[-1]
        sw = jnp.concatenate([_dot_tn(kw_ref[0].astype(BF16), qbd), _dot(kwn_ref[0], qbd)], axis=0)
        jw = lax.broadcasted_iota(jnp.int32, (wbuf + t_new, R), 0)
        dpos = wbuf + i_lane - jw
        p_win = _softmax_cols(sw, (dpos >= 0) & (dpos < WINDOW) & (past - wbuf + jw >= 0)).astype(BF16)
        o_win = _dot(vw_ref[0].astype(BF16), p_win[0:wbuf]) + _dot_tn(vwn_ref[0], p_win[wbuf:wbuf + t_new])
        oc_ref[...] = g_ref[0, 0:1, :] * o_cmp + g_ref[0, 2:3, :] * o_win

        new_blk = past // SEL_BLOCK
        s_new = _dot(kn_ref[0], qbd) + bias_ref[new_blk:new_blk + 1, :]
        ip = lax.broadcasted_iota(jnp.int32, (t_new, R), 0)
        s_new = jnp.where(ip <= i_lane, s_new, NEG)
        m0 = jnp.max(s_new, axis=0, keepdims=True)
        p0 = jnp.exp2(s_new - m0)
        m_ref[...] = m0
        l_ref[...] = jnp.sum(p0, axis=0, keepdims=True)
        acc_ref[...] = _dot_tn(vn_ref[0], p0.astype(BF16))

    blk_per_page = page // SEL_BLOCK
    nb = PAGES_PER_STEP * blk_per_page
    b_t = bias_ref[pl.ds(pl.multiple_of(kt * nb, nb), nb), :]
    scores = []
    for j, p_ref in enumerate(pages):
        s = _dot_tn(p_ref[0, 0].reshape(gd, page).astype(BF16), qbd)
        scores.append(jnp.concatenate(
            [s[jb * SEL_BLOCK:(jb + 1) * SEL_BLOCK] + b_t[j * blk_per_page + jb:j * blk_per_page + jb + 1, :]
             for jb in range(blk_per_page)], axis=0))
    m_old = m_ref[...]
    m_new = m_old
    for s in scores:
        m_new = jnp.maximum(m_new, jnp.max(s, axis=0, keepdims=True))
    alpha = jnp.exp2(m_old - m_new)
    l = alpha * l_ref[...]
    acc = alpha * acc_ref[...]
    for s, p_ref in zip(scores, pages):
        p = jnp.exp2(s - m_new)
        l = l + jnp.sum(p, axis=0, keepdims=True)
        acc = acc + _dot(p_ref[0, 1].reshape(gd, page).astype(BF16), p.astype(BF16))
    m_ref[...] = m_new
    l_ref[...] = l
    acc_ref[...] = acc

    @pl.when(kt == pl.num_programs(1) - 1)
    def _():
        o = oc_ref[...] + g_ref[0, 1:2, :] * (acc / jnp.maximum(l, 1e-30))
        grp = lane // (GROUP * t_new)
        out = jnp.zeros((HEAD_DIM, R), F32)
        for g in range(N_KV):
            out = out + jnp.where(grp == g, o[g * HEAD_DIM:(g + 1) * HEAD_DIM, :], 0.0)
        o_ref[0] = out


def _sample_attn(cache_t, page_table, qbd, gT, kcv, kn, vn, kw, vw, kwn, vwn, mT, rr, *, t_new):
    page = cache_t.shape[-1]
    DB, n_pages = page_table.shape
    past = n_pages * page
    R = qbd.shape[-1]
    per_b = lambda a: pl.BlockSpec((1,) + a.shape[1:], lambda b, kt, pt: (b,) + (0,) * (a.ndim - 1))
    const = lambda a: pl.BlockSpec(a.shape, lambda b, kt, pt: (0,) * a.ndim)
    cmp_spec = lambda st: pl.BlockSpec((None, 1) + kcv.shape[2:], lambda b, kt, pt: (st, b, 0, 0, 0))
    spec = _page_specs(2, 1, page)
    return pl.pallas_call(
        functools.partial(_sample_attn_kernel, t_new=t_new, past=past, page=page),
        grid_spec=pltpu.PrefetchScalarGridSpec(
            num_scalar_prefetch=1,
            grid=(DB, n_pages // PAGES_PER_STEP),
            in_specs=[spec(j) for j in range(PAGES_PER_STEP)]
            + [per_b(qbd), per_b(gT), cmp_spec(0), cmp_spec(1), per_b(kn), per_b(vn), per_b(kw), per_b(vw),
               per_b(kwn), per_b(vwn), const(mT), const(rr)],
            out_specs=pl.BlockSpec((1, HEAD_DIM, R), lambda b, kt, pt: (b, 0, 0)),
            scratch_shapes=[pltpu.VMEM((mT.shape[0], R), F32), pltpu.VMEM((1, R), F32), pltpu.VMEM((1, R), F32),
                            pltpu.VMEM((N_KV * HEAD_DIM, R), F32), pltpu.VMEM((N_KV * HEAD_DIM, R), F32)],
        ),
        out_shape=jax.ShapeDtypeStruct((DB, HEAD_DIM, R), F32),
        compiler_params=_cparams(("arbitrary", "arbitrary")),
        name="sample_attn",
    )(page_table, *([cache_t] * PAGES_PER_STEP), qbd, gT, kcv, kcv, kn, vn, kw, vw, kwn, vwn, mT, rr)


def _nsa_outproj_kernel(o_ref, sz_ref, x_ref, w_ref, g_ref, b_ref, y_ref):
    gated = (o_ref[...] * sz_ref[...]).astype(BF16)
    y = _dot(gated, w_ref[...])
    y_ref[...] = _layer_norm(ALPHA * x_ref[...] + y, g_ref[...], b_ref[...])


def _nsa_outproj(o, sz, x, w, g, b, *, tm):
    N, D = x.shape
    row = lambda i: (i, 0)
    const = lambda i: (0, 0)
    return pl.pallas_call(
        _nsa_outproj_kernel,
        grid=(N // tm,),
        in_specs=[pl.BlockSpec((tm, o.shape[1]), row), pl.BlockSpec((tm, sz.shape[1]), row),
                  pl.BlockSpec((tm, D), row), pl.BlockSpec(w.shape, const),
                  pl.BlockSpec(g.shape, const), pl.BlockSpec(b.shape, const)],
        out_specs=pl.BlockSpec((tm, D), row),
        out_shape=jax.ShapeDtypeStruct((N, D), F32),
        compiler_params=_cparams(("arbitrary",)),
        name="nsa_outproj",
    )(o, sz, x, w, g, b)


def _prep_nsa_weights(w_in, pe, w1, b1, w2):
    D = w_in.shape[0]
    att_w = N_KV * GROUP * HEAD_DIM
    kv_w = 6 * N_KV * HEAD_DIM
    o1, o2 = att_w, att_w + kv_w
    o3 = o2 + 3 * N_KV * GROUP
    w_q = w_in[:, :o1] * (HEAD_DIM ** -0.5 * np.log2(np.e))
    w_g = w_in[:, o2:o3].reshape(D, N_KV, GROUP, 3).transpose(0, 3, 1, 2).reshape(D, 3 * N_KV * GROUP)
    w_g = jnp.pad(w_g, ((0, 0), (0, 128 - w_g.shape[1])))
    w_all = jnp.concatenate([w_q, w_in[:, o1:o2], w_in[:, o3:], w_g], axis=1).astype(BF16)
    kv_cols = w_in[:, o1:o2].reshape(D, 6, N_KV, HEAD_DIM)
    padded = lambda st: jnp.pad(kv_cols[:, st], ((0, 0), (0, 0), (0, K_AUG - HEAD_DIM))).reshape(D, N_KV * K_AUG)
    w_nat = jnp.concatenate([w_q, w_in[:, o3:], w_g, padded(2), padded(4)], axis=1).astype(BF16)
    w_kvt = w_in[:, o1:o2].T.astype(BF16)
    r = CMP_BLOCK // CMP_STRIDE
    pe_rows = jnp.pad(pe.reshape(2, 1, CMP_BLOCK * HEAD_DIM), ((0, 0), (0, 7), (0, 0))).astype(BF16)
    w1_flat = w1.reshape(2, CMP_BLOCK * HEAD_DIM, CMP_HID).astype(BF16)
    w1p = w1.reshape(2, r, CMP_STRIDE // 2, 2, HEAD_DIM, CMP_HID).transpose(0, 2, 3, 4, 1, 5)
    w1p = w1p.reshape(2, CMP_STRIDE // 2, 2, HEAD_DIM, r * CMP_HID)
    eye = jnp.eye(2, dtype=w1.dtype)
    w1_pair = (w1p[:, :, :, None, :, None, :] * eye[None, None, None, :, None, :, None])
    w1_pair = w1_pair.reshape(2, CMP_STRIDE // 2, 4 * HEAD_DIM, 2 * r * CMP_HID).astype(BF16)
    return (w_all, w_nat, w_kvt), pe_rows, w1_flat, w1_pair, b1.reshape(2, 1, CMP_HID), w2.astype(BF16)


def _nsa_prompt(x1, nsa_w, w_out, g, b):
    B, T, D = x1.shape
    (_, w_nat, w_kvt), pe_rows, w1_flat, w1_pair, b1, w2 = nsa_w
    att_w = N_KV * GROUP * HEAD_DIM
    kt_size = 256
    q, sz, gates, ksa, kwa, kvt, va = _nsa_inproj_prompt(x1, w_nat, w_kvt, d_q=att_w, kt_size=kt_size, tm=512)
    kvt6 = kvt.reshape(B, 6, N_KV, HEAD_DIM, T)
    rows = kvt6[:, :4].transpose(0, 4, 1, 2, 3)
    win = kvt6[:, 4:, :, :, T - min(WINDOW, T):].transpose(0, 4, 1, 2, 3)
    nchunk = T // CMP_STRIDE
    kcv = _sample_compress(kvt6, None, pe_rows, w1_flat, w1_pair, b1, w2, name="prompt_compress")
    mT = jnp.asarray(_cmp_to_sel_matrix(T // SEL_BLOCK, nchunk), BF16)
    o = _prompt_attn(q, gates, kcv[0], kcv[1].transpose(0, 1, 3, 2), ksa, kwa, va, mT, tq=128, kt_size=kt_size)
    y = _nsa_outproj(o.reshape(B * T, att_w), sz.reshape(B * T, att_w), x1.reshape(B * T, D), w_out, g, b, tm=512)
    return y.reshape(B, T, D), rows, win


def _nsa_sample(x1, cache_kv_l, cache_win_l, page_table, nsa_w, w_out, g, b):
    DB, T, D = x1.shape
    (w_all, _, _), pe_rows, w1_flat, w1_pair, b1, w2 = nsa_w
    att_w = N_KV * GROUP * HEAD_DIM
    kv_w = 6 * N_KV * HEAD_DIM
    page = cache_kv_l.shape[1]
    past = page_table.shape[1] * page
    assert T <= CMP_STRIDE and past % SEL_BLOCK == 0 and past % CMP_STRIDE == 0
    x_flat = x1.reshape(DB * T, D)
    q, kv, sz, gates = _nsa_inproj(x_flat, w_all, d_q=att_w, d_kv=kv_w, tm=DB * T)
    kv6 = kv.reshape(DB, T, 6, N_KV, HEAD_DIM)
    rows = kv6[:, :, :4]
    wseq = jnp.concatenate([cache_win_l, kv6[:, :, 4:]], axis=1)
    gd = N_KV * HEAD_DIM
    R = N_KV * GROUP * T
    qT = q.reshape(DB, T, N_KV, GROUP, HEAD_DIM).transpose(0, 2, 4, 3, 1)
    eye = jnp.eye(N_KV, dtype=BF16)
    qbd = (qT[:, :, :, None] * eye[None, :, None, :, None, None]).reshape(DB, gd, R)
    gT = gates[:, :3 * N_KV * GROUP].reshape(DB, T, 3, N_KV * GROUP).transpose(0, 2, 3, 1).reshape(DB, 3, R)
    gT = jnp.pad(gT, ((0, 0), (0, 5), (0, 0)))
    new_rows = lambda st: kv6[:, :, st].reshape(DB, T, gd).astype(BF16)
    cache_t = cache_kv_l.transpose(0, 2, 3, 4, 1)
    win_t = cache_win_l.transpose(0, 2, 3, 4, 1)
    wbuf = win_t.shape[-1]
    nchunk = past // CMP_STRIDE
    kcv = _sample_compress(cache_t, page_table, pe_rows, w1_flat, w1_pair, b1, w2)
    n_sel = past // SEL_BLOCK + 1
    n_sel_pad = -(-n_sel // (PAGES_PER_STEP * page // SEL_BLOCK)) * (PAGES_PER_STEP * page // SEL_BLOCK)
    mT = np.zeros((n_sel_pad, nchunk), np.float32)
    mT[:n_sel] = _cmp_to_sel_matrix(n_sel, nchunk)
    lane = np.arange(R)
    rr = ((lane[:, None] % T == lane[None, :] % T)
          & (lane[:, None] // (GROUP * T) == lane[None, :] // (GROUP * T))).astype(np.float32)
    oT = _sample_attn(cache_t, page_table, qbd, gT, kcv, new_rows(2), new_rows(3),
                      win_t[:, 0].reshape(DB, gd, wbuf), win_t[:, 1].reshape(DB, gd, wbuf), new_rows(4), new_rows(5),
                      jnp.asarray(mT, BF16), jnp.asarray(rr, BF16), t_new=T)
    o = oT.reshape(DB, HEAD_DIM, N_KV, GROUP, T).transpose(0, 4, 2, 3, 1).reshape(DB * T, att_w)
    y = _nsa_outproj(o, sz, x_flat, w_out, g, b, tm=DB * T)
    return y.reshape(DB, T, D), rows, wseq[:, T:]


def kernel(x_prompt, x_sample, state_conv, cache_kv, cache_win, page_table, conv_w_in, conv_b_in, conv_dw_w,
           conv_dw_b, conv_ln_g, conv_ln_b, conv_w_out, nsa_w_in, cmp_pe, cmp_w1, cmp_b1, cmp_w2, nsa_w_out,
           ln_g, ln_b):
    B, T, D = x_prompt.shape
    DB, TS, _ = x_sample.shape
    d_in = conv_w_out.shape[1]
    row = lambda v: v.reshape(1, -1)

    cw = (conv_w_in[0].astype(BF16), row(conv_b_in[0]), conv_dw_w[0], row(conv_dw_b[0]), row(conv_ln_g[0]),
          row(conv_ln_b[0]), conv_w_out[0].astype(BF16), row(ln_g[0]), row(ln_b[0]))
    pad = HALO - (CONV_W - 1)
    xp, sp = _conv_layer(x_prompt, jnp.zeros((B, HALO, d_in), F32), *cw, ts=256, carry=True)
    st = jnp.pad(state_conv[0], ((0, 0), (pad, 0), (0, 0)))
    xs, ss = _conv_layer(x_sample, st, *cw, ts=TS, carry=False)

    nsa_w = _prep_nsa_weights(nsa_w_in[0], cmp_pe[0], cmp_w1[0], cmp_b1[0], cmp_w2[0])
    w_out = nsa_w_out[0].astype(BF16)
    yp, rp, wp = _nsa_prompt(xp, nsa_w, w_out, row(ln_g[1]), row(ln_b[1]))
    ys, rs, ws = _nsa_sample(xs, cache_kv[0], cache_win[0], page_table, nsa_w, w_out, row(ln_g[1]), row(ln_b[1]))
    return (yp, ys, sp[None, :, pad:], ss[None, :, pad:], rp[None], rs[None], wp[None], ws[None])
```

```python
import functools

import jax
import jax.numpy as jnp
import numpy as np
from jax import lax
from jax.experimental import pallas as pl
from jax.experimental.pallas import tpu as pltpu

F32 = jnp.float32
BF16 = jnp.bfloat16

CONV_W = 31
N_KV = 4
GROUP = 4
HEAD_DIM = 64
CMP_BLOCK = 32
CMP_STRIDE = 16
CMP_HID = 128
SEL_BLOCK = 64
N_SELECT = 16
WINDOW = 512
DEPTH = 2
ALPHA = (2 * DEPTH) ** 0.25
LN_EPS = 1e-5
NEG = -1e30

VMEM_LIMIT = 56 * 1024 * 1024
LANES = 128
HALO = 32


def _cparams(sem):
    return pltpu.CompilerParams(dimension_semantics=sem, vmem_limit_bytes=VMEM_LIMIT)


def _sigmoid(x):
    return 1.0 / (1.0 + jnp.exp(-x))


def _layer_norm(x, g, b):
    mu = jnp.mean(x, axis=-1, keepdims=True)
    xc = x - mu
    var = jnp.mean(xc * xc, axis=-1, keepdims=True)
    return xc * lax.rsqrt(var + LN_EPS) * g + b


def _dot(a, b):
    return jnp.dot(a, b, preferred_element_type=F32)


def _conv_layer_kernel(x_ref, st_ref, win_ref, bin_ref, dww_ref, dwb_ref, clg_ref, clb_ref, wout_ref,
                       lng_ref, lnb_ref, y_ref, ns_ref, ubuf, zbuf, cbuf, *, ns, ts, d_in, carry):
    t = pl.program_id(1)
    rows = ns * ts
    cb = LANES
    n_cb = d_in // cb
    lanes = lambda ci: slice(ci * cb, (ci + 1) * cb)

    if carry:
        @pl.when(t == 0)
        def _():
            ubuf[:, :, 0:HALO, :] = jnp.zeros((ns, n_cb, HALO, cb), F32)
    else:
        for ci in range(n_cb):
            ubuf[:, ci, 0:HALO, :] = st_ref[:, :, lanes(ci)]

    x = x_ref[...].reshape(rows, x_ref.shape[-1])
    xb = x.astype(BF16)
    a = _dot(xb, win_ref[:, 0:d_in]) + bin_ref[:, 0:d_in]
    gl = _dot(xb, win_ref[:, d_in:2 * d_in]) + bin_ref[:, d_in:2 * d_in]
    u = (a * _sigmoid(gl)).reshape(ns, ts, d_in)
    for ci in range(n_cb):
        ubuf[:, ci, HALO:HALO + ts, :] = u[:, :, lanes(ci)]
    z = _dot(xb, win_ref[:, 2 * d_in:3 * d_in]) + bin_ref[:, 2 * d_in:3 * d_in]
    zbuf[...] = z * _sigmoid(z)

    rb = min(ts, 64)
    n_rb = ts // rb
    off0 = HALO - (CONV_W - 1)

    def chunk(i, c):
        s = i // (n_rb * n_cb)
        rem = i % (n_rb * n_cb)
        r0 = pl.multiple_of((rem // n_cb) * rb, rb)
        ci = rem % n_cb
        c0 = pl.multiple_of(ci * cb, cb)
        acc = jnp.zeros((rb, cb), F32) + dwb_ref[:, pl.ds(c0, cb)]
        for k in range(CONV_W):
            acc = acc + dww_ref[k:k + 1, pl.ds(c0, cb)] * ubuf[s, ci, pl.ds(r0 + off0 + k, rb, stride=1), :]
        cbuf[s, pl.ds(r0, rb), pl.ds(c0, cb)] = acc
        return c

    lax.fori_loop(0, ns * n_rb * n_cb, chunk, 0)

    for ci in range(n_cb):
        ns_ref[:, :, lanes(ci)] = ubuf[:, ci, ts:ts + HALO, :]
    if carry:
        ubuf[:, :, 0:HALO, :] = ubuf[:, :, ts:ts + HALO, :]

    yc = _layer_norm(cbuf[...].reshape(rows, d_in), clg_ref[...], clb_ref[...])
    gated = (yc * _sigmoid(yc)) * zbuf[...]
    out = _dot(gated.astype(BF16), wout_ref[...])
    y = _layer_norm(ALPHA * x + out, lng_ref[...], lnb_ref[...])
    y_ref[...] = y.reshape(y_ref.shape)


def _conv_layer(x, state, w_in, b_in, dw_w, dw_b, cl_g, cl_b, w_out, ln_g, ln_b, *, ts, carry):
    S, T, D = x.shape
    d_in = w_out.shape[0]
    ns = 1 if carry else S
    grid = (S // ns, T // ts)
    const = lambda s, t: (0, 0)
    kernel = functools.partial(_conv_layer_kernel, ns=ns, ts=ts, d_in=d_in, carry=carry)
    y, new_state = pl.pallas_call(
        kernel,
        grid=grid,
        in_specs=[
            pl.BlockSpec((ns, ts, D), lambda s, t: (s, t, 0)),
            pl.BlockSpec((ns, HALO, d_in), lambda s, t: (s, 0, 0)),
            pl.BlockSpec(w_in.shape, const),
            pl.BlockSpec(b_in.shape, const),
            pl.BlockSpec(dw_w.shape, const),
            pl.BlockSpec(dw_b.shape, const),
            pl.BlockSpec(cl_g.shape, const),
            pl.BlockSpec(cl_b.shape, const),
            pl.BlockSpec(w_out.shape, const),
            pl.BlockSpec(ln_g.shape, const),
            pl.BlockSpec(ln_b.shape, const),
        ],
        out_specs=[
            pl.BlockSpec((ns, ts, D), lambda s, t: (s, t, 0)),
            pl.BlockSpec((ns, HALO, d_in), lambda s, t: (s, 0, 0)),
        ],
        out_shape=[
            jax.ShapeDtypeStruct((S, T, D), F32),
            jax.ShapeDtypeStruct((S, HALO, d_in), F32),
        ],
        scratch_shapes=[
            pltpu.VMEM((ns, d_in // LANES, HALO + ts, LANES), F32),
            pltpu.VMEM((ns * ts, d_in), F32),
            pltpu.VMEM((ns, ts, d_in), F32),
        ],
        compiler_params=_cparams(("arbitrary", "arbitrary")),
        name="conv_layer_carry" if carry else "conv_layer_state",
    )(x, state, w_in, b_in, dw_w, dw_b, cl_g, cl_b, w_out, ln_g, ln_b)
    return y, new_state


def _nsa_inproj_kernel(x_ref, w_ref, q_ref, kv_ref, sz_ref, g_ref, *, d_q, d_kv):
    xb = x_ref[...].astype(BF16)
    q_ref[...] = _dot(xb, w_ref[:, 0:d_q]).astype(BF16)
    kv_ref[...] = _dot(xb, w_ref[:, d_q:d_q + d_kv])
    z = _dot(xb, w_ref[:, d_q + d_kv:2 * d_q + d_kv])
    sz_ref[...] = z * _sigmoid(z)
    g_ref[...] = _sigmoid(_dot(xb, w_ref[:, 2 * d_q + d_kv:]))


def _nsa_inproj(x, w, *, d_q, d_kv, tm):
    N, D = x.shape
    d_g = w.shape[1] - 2 * d_q - d_kv
    row = lambda i: (i, 0)
    return pl.pallas_call(
        functools.partial(_nsa_inproj_kernel, d_q=d_q, d_kv=d_kv),
        grid=(N // tm,),
        in_specs=[pl.BlockSpec((tm, D), row), pl.BlockSpec(w.shape, lambda i: (0, 0))],
        out_specs=[pl.BlockSpec((tm, d_q), row), pl.BlockSpec((tm, d_kv), row),
                   pl.BlockSpec((tm, d_q), row), pl.BlockSpec((tm, d_g), row)],
        out_shape=[jax.ShapeDtypeStruct((N, d_q), BF16), jax.ShapeDtypeStruct((N, d_kv), F32),
                   jax.ShapeDtypeStruct((N, d_q), F32), jax.ShapeDtypeStruct((N, d_g), F32)],
        compiler_params=_cparams(("arbitrary",)),
        name="nsa_inproj",
    )(x, w)


def _nsa_inproj_prompt_kernel(x_ref, wn_ref, wt_ref, q_ref, sz_ref, g_ref, ksa_ref, kwa_ref, kvt_ref, va_ref,
                              *, d_q, kt_size):
    tm = x_ref.shape[1]
    t0 = pl.program_id(1) * tm
    xb = x_ref[0].astype(BF16)
    q_ref[0] = _dot(xb, wn_ref[:, 0:d_q]).astype(BF16)
    z = _dot(xb, wn_ref[:, d_q:2 * d_q])
    sz_ref[0] = z * _sigmoid(z)
    o = 2 * d_q
    g_ref[0] = _sigmoid(_dot(xb, wn_ref[:, o:o + LANES]))
    o += LANES
    kw = N_KV * K_AUG
    pos = t0 + lax.broadcasted_iota(jnp.int32, (tm, kw), 0)
    lane = lax.broadcasted_iota(jnp.int32, (tm, kw), 1) % K_AUG
    onehot = jnp.where(lane - HEAD_DIM == (pos % kt_size) // SEL_BLOCK, 1.0, 0.0)
    ksa_ref[0] = (_dot(xb, wn_ref[:, o:o + kw]) + onehot).astype(BF16)
    kwa_ref[0] = _dot(xb, wn_ref[:, o + kw:o + 2 * kw]).astype(BF16)
    hT = lax.dot_general(wt_ref[...], xb, (((1,), (1,)), ((), ())), preferred_element_type=F32)
    kvt_ref[0] = hT
    ones = jnp.concatenate([jnp.ones((1, tm), F32), jnp.zeros((V_AUG - HEAD_DIM - 1, tm), F32)], axis=0).astype(BF16)
    for i, st in enumerate((3, 5)):
        for g in range(N_KV):
            r0 = (st * N_KV + g) * HEAD_DIM
            va_ref[0, i, g, 0:HEAD_DIM, :] = hT[r0:r0 + HEAD_DIM].astype(BF16)
            va_ref[0, i, g, HEAD_DIM:V_AUG, :] = ones


def _nsa_inproj_prompt(x, w_nat, w_kvt, *, d_q, kt_size, tm):
    B, T, D = x.shape
    kw = N_KV * K_AUG
    tok = lambda n: pl.BlockSpec((1, tm, n), lambda b, t: (b, t, 0))
    const = lambda a: pl.BlockSpec(a.shape, lambda b, t: (0, 0))
    return pl.pallas_call(
        functools.partial(_nsa_inproj_prompt_kernel, d_q=d_q, kt_size=kt_size),
        grid=(B, T // tm),
        in_specs=[tok(D), const(w_nat), const(w_kvt)],
        out_specs=[tok(d_q), tok(d_q), tok(LANES), tok(kw), tok(kw),
                   pl.BlockSpec((1, w_kvt.shape[0], tm), lambda b, t: (b, 0, t)),
                   pl.BlockSpec((1, 2, N_KV, V_AUG, tm), lambda b, t: (b, 0, 0, 0, t))],
        out_shape=[jax.ShapeDtypeStruct((B, T, d_q), BF16), jax.ShapeDtypeStruct((B, T, d_q), F32),
                   jax.ShapeDtypeStruct((B, T, LANES), F32), jax.ShapeDtypeStruct((B, T, kw), BF16),
                   jax.ShapeDtypeStruct((B, T, kw), BF16), jax.ShapeDtypeStruct((B, w_kvt.shape[0], T), F32),
                   jax.ShapeDtypeStruct((B, 2, N_KV, V_AUG, T), BF16)],
        compiler_params=_cparams(("arbitrary", "arbitrary")),
        name="nsa_inproj_prompt",
    )(x, w_nat, w_kvt)


def _compress_finish(parts, base, w2):
    nchunk = parts.shape[0]
    second = pltpu.roll(parts[:, CMP_HID:], nchunk - 1, 0)
    row = lax.broadcasted_iota(jnp.int32, (nchunk, CMP_HID), 0)
    second = jnp.where(row < nchunk - 1, second, 0.0)
    h = base + parts[:, :CMP_HID] + second
    act = h * _sigmoid(h)
    return _dot(act.astype(BF16), w2).astype(BF16)


def _dot_f32_by_const(x, c, *, const_first):
    hi = x.astype(BF16)
    r1 = x - hi.astype(F32)
    mid = r1.astype(BF16)
    lo = (r1 - mid.astype(F32)).astype(BF16)
    out = None
    for part in (hi, mid, lo):
        d = _dot(c, part) if const_first else _dot(part, c)
        out = d if out is None else out + d
    return out


def _softmax_cols(s, valid):
    s = jnp.where(valid, s, NEG)
    m = jnp.max(s, axis=0, keepdims=True)
    e = jnp.where(valid, jnp.exp2(s - m), 0.0)
    return e / jnp.maximum(jnp.sum(e, axis=0, keepdims=True), 1e-30)


def _selection_scores(selimp, cur):
    j = lax.broadcasted_iota(jnp.int32, selimp.shape, 0)
    causal = j <= cur
    forced = (j == 0) | (j == cur) | (j == cur - 1)
    return jnp.where(forced, jnp.inf, jnp.where(causal, selimp, -jnp.inf)), causal


def _select_blocks(selimp, cur, n_select):
    ns = selimp.shape[0]
    j = lax.broadcasted_iota(jnp.int32, selimp.shape, 0)
    score, causal = _selection_scores(selimp, cur)
    taken = jnp.zeros(selimp.shape, jnp.int32)
    for _ in range(n_select):
        m = jnp.max(score, axis=0, keepdims=True)
        first = jnp.min(jnp.where(score == m, j, ns), axis=0, keepdims=True)
        hit = j == first
        taken = jnp.where(hit, 1, taken)
        score = jnp.where(hit, -jnp.inf, score)
    return (taken > 0) & causal


def _select_blocks_by_rank(selimp, cur, n_select, score_ref):
    ns = selimp.shape[0]
    score, causal = _selection_scores(selimp, cur)
    score_ref[...] = score
    sub = lax.broadcasted_iota(jnp.int32, (8, selimp.shape[1]), 0)
    ranks = []
    for v in range(ns // 8):
        blk = score[8 * v:8 * v + 8]
        rank = jnp.zeros(blk.shape, F32)
        for jp in range(ns):
            row = score_ref[jp:jp + 1, :]
            if jp < 8 * v:
                ahead = row >= blk
            elif jp >= 8 * v + 8:
                ahead = row > blk
            else:
                ahead = (row > blk) | ((row == blk) & (sub > jp - 8 * v))
            rank = rank + jnp.where(ahead, 1.0, 0.0)
        ranks.append(rank)
    return (jnp.concatenate(ranks, axis=0) < n_select) & causal


def _cmp_to_sel_matrix(n_sel, n_cmp):
    r = SEL_BLOCK // CMP_STRIDE
    lo = -(CMP_BLOCK // CMP_STRIDE - 1)
    m = np.zeros((n_sel, n_cmp), np.float32)
    for o in range(lo, r):
        start = o * CMP_STRIDE
        ov = max(0, min(SEL_BLOCK, start + CMP_BLOCK) - max(0, start))
        for jb in range(n_sel):
            n = r * jb + o
            if ov > 0 and 0 <= n < n_cmp:
                m[jb, n] += ov / CMP_BLOCK
    return m


V_AUG = 80
K_AUG = 128
BIAS_ROWS = 16


def _prompt_attn_kernel(q_ref, g_ref, kc_ref, vct_ref, ksa_ref, vsa_ref, kwa_ref, vwa_ref, mt_ref, o_ref,
                        bias_ref, score_ref, *, tq, kt_size, gps):
    qi = pl.program_id(1)
    qs = qi * tq
    hp = GROUP // 2
    RH = 2 * tq
    n_cmp = kc_ref.shape[2]
    n_sel = mt_ref.shape[0]
    blk_per_tile = kt_size // SEL_BLOCK
    chains = [(gi, h) for gi in range(gps) for h in range(hp)]

    t1 = qs + lax.broadcasted_iota(jnp.int32, (1, tq), 1)
    t_lane = jnp.concatenate([t1, t1], axis=1)
    pair = lambda gi, h: slice((gi * hp + h) * 2 * HEAD_DIM, (gi * hp + h + 1) * 2 * HEAD_DIM)
    qT = {}
    for gi, h in chains:
        t2 = q_ref[0, :, pair(gi, h)].astype(F32).T.astype(BF16)
        qT[gi, h] = jnp.concatenate([t2[0:HEAD_DIM], t2[HEAD_DIM:2 * HEAD_DIM]], axis=1)
    gT = g_ref[0].T
    k_lanes = lambda gi: slice(gi * K_AUG, (gi + 1) * K_AUG)

    wk = WINDOW + tq
    start = pl.multiple_of(jnp.maximum(qs - WINDOW, 0), 128)
    s_cmp = {c: _dot(kc_ref[0, c[0]], qT[c]) for c in chains}
    k_pad = jnp.zeros((K_AUG - HEAD_DIM, RH), BF16)
    s_win = {c: _dot(kwa_ref[0, pl.ds(start, wk), k_lanes(c[0])], jnp.concatenate([qT[c], k_pad], axis=0))
             for c in chains}

    n_idx = lax.broadcasted_iota(jnp.int32, (n_cmp, RH), 0)
    cmp_valid = n_idx * CMP_STRIDE + (CMP_BLOCK - 1) <= t_lane
    o_cmp = {}
    imps = [None] * gps
    for gi, h in chains:
        p_cmp = _softmax_cols(s_cmp[gi, h], cmp_valid)
        o_cmp[gi, h] = _dot(vct_ref[0, gi], p_cmp.astype(BF16))
        part = p_cmp[:, 0:tq] + p_cmp[:, tq:2 * tq]
        imps[gi] = part if imps[gi] is None else imps[gi] + part

    dpos = t_lane - (start + lax.broadcasted_iota(jnp.int32, (wk, RH), 0))
    win_valid = (dpos >= 0) & (dpos < WINDOW)
    o_win = {}
    for gi, h in chains:
        sw = jnp.where(win_valid, s_win[gi, h], NEG)
        p_win = jnp.exp2(sw - jnp.max(sw, axis=0, keepdims=True)).astype(BF16)
        acc_win = _dot(vwa_ref[0, 0, gi, :, pl.ds(start, wk)], p_win)
        o_win[gi, h] = acc_win[0:HEAD_DIM] / jnp.maximum(acc_win[HEAD_DIM:HEAD_DIM + 1], 1e-30)

    pad = jnp.zeros((BIAS_ROWS - blk_per_tile, tq), F32)
    for gi in range(gps):
        selimp = _dot_f32_by_const(imps[gi], mt_ref[...], const_first=True)
        sel = _select_blocks_by_rank(selimp, t1 // SEL_BLOCK, N_SELECT, score_ref.at[gi])
        bias = jnp.where(sel, 0.0, NEG)
        rows = []
        for kt in range(n_sel // blk_per_tile):
            rows += [bias[kt * blk_per_tile:(kt + 1) * blk_per_tile], pad]
        bias_rows = jnp.concatenate(rows, axis=0)
        bias_ref[gi] = jnp.concatenate([bias_rows, bias_rows], axis=1).astype(BF16)

    q_pad = jnp.zeros((K_AUG - HEAD_DIM - BIAS_ROWS, RH), BF16)

    def sel_units(kt, carries, diagonal):
        k0 = pl.multiple_of(kt * kt_size, kt_size)
        scores = []
        for gi, h in chains:
            b_t = bias_ref[gi, pl.ds(pl.multiple_of(kt * BIAS_ROWS, BIAS_ROWS), BIAS_ROWS), :]
            qa = jnp.concatenate([qT[gi, h], b_t, q_pad], axis=0)
            scores.append(_dot(ksa_ref[0, pl.ds(k0, kt_size), k_lanes(gi)], qa))
        out = []
        for (gi, h), s, (m, acc) in zip(chains, scores, carries):
            if diagonal:
                kidx = k0 + lax.broadcasted_iota(jnp.int32, (kt_size, RH), 0)
                s = jnp.where(kidx <= t_lane, s, NEG)
            m_new = jnp.maximum(m, jnp.max(s, axis=0, keepdims=True))
            p = jnp.exp2(s - m_new).astype(BF16)
            acc = jnp.exp2(m - m_new) * acc + _dot(vsa_ref[0, 0, gi, :, pl.ds(k0, kt_size)], p)
            out.append((m_new, acc))
        return tuple(out)

    n_full = qs // kt_size
    init = tuple((jnp.full((1, RH), NEG, F32), jnp.zeros((V_AUG, RH), F32)) for _ in chains)
    carries = lax.fori_loop(0, n_full, lambda kt, c: sel_units(kt, c, False), init)
    carries = sel_units(n_full, carries, True)

    def gate(gi, h, branch):
        r0 = branch * N_KV * GROUP + gi * GROUP + 2 * h
        return jnp.concatenate([gT[r0:r0 + 1], gT[r0 + 1:r0 + 2]], axis=1)

    for i, (gi, h) in enumerate(chains):
        acc_sel = carries[i][1]
        o_sel = acc_sel[0:HEAD_DIM] / jnp.maximum(acc_sel[HEAD_DIM:HEAD_DIM + 1], 1e-30)
        o = gate(gi, h, 0) * o_cmp[gi, h] + gate(gi, h, 1) * o_sel + gate(gi, h, 2) * o_win[gi, h]
        o_ref[0, :, pair(gi, h)] = jnp.concatenate([o[:, 0:tq], o[:, tq:2 * tq]], axis=0).T


def _prompt_attn(q, gates, kc, vcT, ksa, kwa, va, mT, *, tq, kt_size):
    B, T, att_w = q.shape
    G, NC, Dh = kc.shape[1:]
    n_sel = mT.shape[0]
    assert kt_size % SEL_BLOCK == 0 and kt_size // SEL_BLOCK <= BIAS_ROWS and T % kt_size == 0 and kt_size % tq == 0
    assert tq == LANES and G == N_KV
    whole = lambda a: pl.BlockSpec((1,) + a.shape[1:], lambda b, i: (b,) + (0,) * (a.ndim - 1))
    v_spec = lambda s: pl.BlockSpec((1, 1, G, V_AUG, T), lambda b, i: (b, s, 0, 0, 0))
    return pl.pallas_call(
        functools.partial(_prompt_attn_kernel, tq=tq, kt_size=kt_size, gps=G),
        grid=(B, T // tq),
        in_specs=[
            pl.BlockSpec((1, tq, att_w), lambda b, i: (b, i, 0)),
            pl.BlockSpec((1, tq, gates.shape[2]), lambda b, i: (b, i, 0)),
            whole(kc), whole(vcT), whole(ksa), v_spec(0), whole(kwa), v_spec(1),
            pl.BlockSpec(mT.shape, lambda b, i: (0, 0)),
        ],
        out_specs=pl.BlockSpec((1, tq, att_w), lambda b, i: (b, i, 0)),
        out_shape=jax.ShapeDtypeStruct((B, T, att_w), F32),
        scratch_shapes=[pltpu.VMEM((G, T // kt_size * BIAS_ROWS, 2 * tq), BF16),
                        pltpu.VMEM((G, n_sel, tq), F32)],
        compiler_params=_cparams(("arbitrary", "arbitrary")),
        name="prompt_attn",
    )(q, gates, kc, vcT, ksa, va, kwa, va, mT)


def _dot_tn(v, p):
    return lax.dot_general(v, p, (((0,), (0,)), ((), ())), preferred_element_type=F32)


PAGES_PER_STEP = 16


def _page_specs(block_streams, stream_block, page):
    def spec(j):
        return pl.BlockSpec((1, block_streams, N_KV, HEAD_DIM, page),
                            lambda b, kt, pt: (pt[b, kt * PAGES_PER_STEP + j], stream_block, 0, 0, 0))
    return spec


def _sample_compress_kernel(pt_ref, *refs, page):
    del pt_ref
    pages = refs[:PAGES_PER_STEP]
    pe_ref, w1_ref, wp_ref, b1_ref, w2_ref, o_ref, panel, xs = refs[PAGES_PER_STEP:]
    kt = pl.program_id(1)
    cps = (page // CMP_STRIDE) * PAGES_PER_STEP
    row0 = pl.multiple_of(kt * cps, cps)
    for st in range(2):
        for gp in range(N_KV // 2):
            for j, p_ref in enumerate(pages):
                tile = p_ref[0, st, 2 * gp:2 * gp + 2].reshape(2 * HEAD_DIM, page)
                panel[j * page:(j + 1) * page, :] = tile.T
            for s in range(CMP_STRIDE):
                xs[st, gp, s, pl.ds(row0, cps), :] = panel[pl.ds(s, cps, stride=CMP_STRIDE), :].astype(BF16)

    @pl.when(kt == pl.num_programs(1) - 1)
    def _():
        for st in range(2):
            base = _dot(pe_ref[st], w1_ref[st])[0:1, :] + b1_ref[st]
            for gp in range(N_KV // 2):
                lhs = jnp.concatenate([xs[st, gp, s] for s in range(CMP_STRIDE)], axis=1)
                acc = _dot(lhs, wp_ref[st])
                for g2 in range(2):
                    parts = acc[:, g2 * 2 * CMP_HID:(g2 + 1) * 2 * CMP_HID]
                    o_ref[st, 0, 2 * gp + g2] = _compress_finish(parts, base, w2_ref[st])


def _sample_compress(cache_t, page_table, pe, w1, wp, b1, w2, *, name="sample_compress"):
    page = 2 * HEAD_DIM
    if page_table is None:
        DB, n_pages = cache_t.shape[0], cache_t.shape[-1] // page
        page_table = jnp.zeros((1, 1), jnp.int32)

        def spec(j):
            return pl.BlockSpec((1, 2, N_KV, HEAD_DIM, page), lambda b, kt, pt: (b, 0, 0, 0, kt * PAGES_PER_STEP + j))
    else:
        DB, n_pages = page_table.shape
        assert cache_t.shape[-1] == page
        spec = _page_specs(2, 0, page)
    nchunk = n_pages * page // CMP_STRIDE
    assert n_pages % PAGES_PER_STEP == 0 and page == LANES
    const = lambda a: pl.BlockSpec(a.shape, lambda b, kt, pt: (0,) * a.ndim)
    return pl.pallas_call(
        functools.partial(_sample_compress_kernel, page=page),
        grid_spec=pltpu.PrefetchScalarGridSpec(
            num_scalar_prefetch=1,
            grid=(DB, n_pages // PAGES_PER_STEP),
            in_specs=[spec(j) for j in range(PAGES_PER_STEP)] + [const(pe), const(w1), const(wp), const(b1), const(w2)],
            out_specs=pl.BlockSpec((2, 1, N_KV, nchunk, HEAD_DIM), lambda b, kt, pt: (0, b, 0, 0, 0)),
            scratch_shapes=[pltpu.VMEM((PAGES_PER_STEP * page, 2 * HEAD_DIM), F32),
                            pltpu.VMEM((2, N_KV // 2, CMP_STRIDE, nchunk, 2 * HEAD_DIM), BF16)],
        ),
        out_shape=jax.ShapeDtypeStruct((2, DB, N_KV, nchunk, HEAD_DIM), BF16),
        compiler_params=_cparams(("arbitrary", "arbitrary")),
        name=name,
    )(page_table, *([cache_t] * PAGES_PER_STEP), pe, w1, wp, b1, w2)


def _sample_attn_kernel(pt_ref, *refs, t_new, past, page):
    del pt_ref
    pages = refs[:PAGES_PER_STEP]
    (q_ref, g_ref, kc_ref, vc_ref, kn_ref, vn_ref, kw_ref, vw_ref, kwn_ref, vwn_ref, mt_ref, rr_ref,
     o_ref, bias_ref, m_ref, l_ref, acc_ref, oc_ref) = refs[PAGES_PER_STEP:]
    kt = pl.program_id(1)
    gd = N_KV * HEAD_DIM
    R = N_KV * GROUP * t_new
    qbd = q_ref[0]
    lane = lax.broadcasted_iota(jnp.int32, (1, R), 1)
    i_lane = lane % t_new
    t_lane = past + i_lane

    @pl.when(kt == 0)
    def _():
        n_cmp = kc_ref.shape[2]
        sc = None
        for g in range(N_KV):
            d = _dot(kc_ref[0, g], qbd[g * HEAD_DIM:(g + 1) * HEAD_DIM, :])
            sc = d if sc is None else sc + d
        n_idx = lax.broadcasted_iota(jnp.int32, (n_cmp, R), 0)
        p_cmp = _softmax_cols(sc, n_idx * CMP_STRIDE + (CMP_BLOCK - 1) <= t_lane)
        pb = p_cmp.astype(BF16)
        o_cmp = jnp.concatenate([_dot_tn(vc_ref[0, g], pb) for g in range(N_KV)], axis=0)

        imp = _dot_f32_by_const(p_cmp, rr_ref[...], const_first=False)
        selimp = _dot_f32_by_const(imp, mt_ref[...], const_first=True)
        sel = _select_blocks(selimp, t_lane // SEL_BLOCK, N_SELECT)
        bias_ref[...] = jnp.where(sel, 0.0, NEG)

        wbuf = kw_ref.shape[-1]
        sw = jnp.concatenate([_dot_tn(kw_ref[0].astype(BF16), qbd), _dot(kwn_ref[0], qbd)], axis=0)
        jw = lax.broadcasted_iota(jnp.int32, (wbuf + t_new, R), 0)
        dpos = wbuf + i_lane - jw
        p_win = _softmax_cols(sw, (dpos >= 0) & (dpos < WINDOW) & (past - wbuf + jw >= 0)).astype(BF16)
        o_win = _dot(vw_ref[0].astype(BF16), p_win[0:wbuf]) + _dot_tn(vwn_ref[0], p_win[wbuf:wbuf + t_new])
        oc_ref[...] = g_ref[0, 0:1, :] * o_cmp + g_ref[0, 2:3, :] * o_win

        new_blk = past // SEL_BLOCK
        s_new = _dot(kn_ref[0], qbd) + bias_ref[new_blk:new_blk + 1, :]
        ip = lax.broadcasted_iota(jnp.int32, (t_new, R), 0)
        s_new = jnp.where(ip <= i_lane, s_new, NEG)
        m0 = jnp.max(s_new, axis=0, keepdims=True)
        p0 = jnp.exp2(s_new - m0)
        m_ref[...] = m0
        l_ref[...] = jnp.sum(p0, axis=0, keepdims=True)
        acc_ref[...] = _dot_tn(vn_ref[0], p0.astype(BF16))

    blk_per_page = page // SEL_BLOCK
    nb = PAGES_PER_STEP * blk_per_page
    b_t = bias_ref[pl.ds(pl.multiple_of(kt * nb, nb), nb), :]
    kT = jnp.concatenate([p_ref[0, 0].reshape(gd, page).astype(BF16) for p_ref in pages], axis=1)
    vT = jnp.concatenate([p_ref[0, 1].reshape(gd, page).astype(BF16) for p_ref in pages], axis=1)
    half = kT.shape[1] // 2
    s = jnp.concatenate([_dot_tn(kT[:, 0:half], qbd), _dot_tn(kT[:, half:], qbd)], axis=0)
    s = jnp.concatenate([s[jb * SEL_BLOCK:(jb + 1) * SEL_BLOCK] + b_t[jb:jb + 1, :] for jb in range(nb)], axis=0)
    m_old = m_ref[...]
    m_new = jnp.maximum(m_old, jnp.max(s, axis=0, keepdims=True))
    alpha = jnp.exp2(m_old - m_new)
    p = jnp.exp2(s - m_new)
    l = alpha * l_ref[...] + jnp.sum(p, axis=0, keepdims=True)
    pb = p.astype(BF16)
    acc = alpha * acc_ref[...] + jnp.concatenate([_dot(vT[0:gd // 2], pb), _dot(vT[gd // 2:], pb)], axis=0)
    m_ref[...] = m_new
    l_ref[...] = l
    acc_ref[...] = acc

    @pl.when(kt == pl.num_programs(1) - 1)
    def _():
        o = oc_ref[...] + g_ref[0, 1:2, :] * (acc / jnp.maximum(l, 1e-30))
        grp = lane // (GROUP * t_new)
        out = jnp.zeros((HEAD_DIM, R), F32)
        for g in range(N_KV):
            out = out + jnp.where(grp == g, o[g * HEAD_DIM:(g + 1) * HEAD_DIM, :], 0.0)
        o_ref[0] = out


def _sample_attn(cache_t, page_table, qbd, gT, kcv, kn, vn, kw, vw, kwn, vwn, mT, rr, *, t_new):
    page = cache_t.shape[-1]
    DB, n_pages = page_table.shape
    past = n_pages * page
    R = qbd.shape[-1]
    per_b = lambda a: pl.BlockSpec((1,) + a.shape[1:], lambda b, kt, pt: (b,) + (0,) * (a.ndim - 1))
    const = lambda a: pl.BlockSpec(a.shape, lambda b, kt, pt: (0,) * a.ndim)
    cmp_spec = lambda st: pl.BlockSpec((None, 1) + kcv.shape[2:], lambda b, kt, pt: (st, b, 0, 0, 0))
    spec = _page_specs(2, 1, page)
    return pl.pallas_call(
        functools.partial(_sample_attn_kernel, t_new=t_new, past=past, page=page),
        grid_spec=pltpu.PrefetchScalarGridSpec(
            num_scalar_prefetch=1,
            grid=(DB, n_pages // PAGES_PER_STEP),
            in_specs=[spec(j) for j in range(PAGES_PER_STEP)]
            + [per_b(qbd), per_b(gT), cmp_spec(0), cmp_spec(1), per_b(kn), per_b(vn), per_b(kw), per_b(vw),
               per_b(kwn), per_b(vwn), const(mT), const(rr)],
            out_specs=pl.BlockSpec((1, HEAD_DIM, R), lambda b, kt, pt: (b, 0, 0)),
            scratch_shapes=[pltpu.VMEM((mT.shape[0], R), F32), pltpu.VMEM((1, R), F32), pltpu.VMEM((1, R), F32),
                            pltpu.VMEM((N_KV * HEAD_DIM, R), F32), pltpu.VMEM((N_KV * HEAD_DIM, R), F32)],
        ),
        out_shape=jax.ShapeDtypeStruct((DB, HEAD_DIM, R), F32),
        compiler_params=_cparams(("arbitrary", "arbitrary")),
        name="sample_attn",
    )(page_table, *([cache_t] * PAGES_PER_STEP), qbd, gT, kcv, kcv, kn, vn, kw, vw, kwn, vwn, mT, rr)


def _nsa_outproj_kernel(o_ref, sz_ref, x_ref, w_ref, g_ref, b_ref, y_ref):
    gated = (o_ref[...] * sz_ref[...]).astype(BF16)
    y = _dot(gated, w_ref[...])
    y_ref[...] = _layer_norm(ALPHA * x_ref[...] + y, g_ref[...], b_ref[...])


def _nsa_outproj(o, sz, x, w, g, b, *, tm):
    N, D = x.shape
    row = lambda i: (i, 0)
    const = lambda i: (0, 0)
    return pl.pallas_call(
        _nsa_outproj_kernel,
        grid=(N // tm,),
        in_specs=[pl.BlockSpec((tm, o.shape[1]), row), pl.BlockSpec((tm, sz.shape[1]), row),
                  pl.BlockSpec((tm, D), row), pl.BlockSpec(w.shape, const),
                  pl.BlockSpec(g.shape, const), pl.BlockSpec(b.shape, const)],
        out_specs=pl.BlockSpec((tm, D), row),
        out_shape=jax.ShapeDtypeStruct((N, D), F32),
        compiler_params=_cparams(("arbitrary",)),
        name="nsa_outproj",
    )(o, sz, x, w, g, b)


def _prep_nsa_weights(w_in, pe, w1, b1, w2):
    D = w_in.shape[0]
    att_w = N_KV * GROUP * HEAD_DIM
    kv_w = 6 * N_KV * HEAD_DIM
    o1, o2 = att_w, att_w + kv_w
    o3 = o2 + 3 * N_KV * GROUP
    w_q = w_in[:, :o1] * (HEAD_DIM ** -0.5 * np.log2(np.e))
    w_g = w_in[:, o2:o3].reshape(D, N_KV, GROUP, 3).transpose(0, 3, 1, 2).reshape(D, 3 * N_KV * GROUP)
    w_g = jnp.pad(w_g, ((0, 0), (0, 128 - w_g.shape[1])))
    w_all = jnp.concatenate([w_q, w_in[:, o1:o2], w_in[:, o3:], w_g], axis=1).astype(BF16)
    kv_cols = w_in[:, o1:o2].reshape(D, 6, N_KV, HEAD_DIM)
    padded = lambda st: jnp.pad(kv_cols[:, st], ((0, 0), (0, 0), (0, K_AUG - HEAD_DIM))).reshape(D, N_KV * K_AUG)
    w_nat = jnp.concatenate([w_q, w_in[:, o3:], w_g, padded(2), padded(4)], axis=1).astype(BF16)
    w_kvt = w_in[:, o1:o2].T.astype(BF16)
    r = CMP_BLOCK // CMP_STRIDE
    pe_rows = jnp.pad(pe.reshape(2, 1, CMP_BLOCK * HEAD_DIM), ((0, 0), (0, 7), (0, 0))).astype(BF16)
    w1_flat = w1.reshape(2, CMP_BLOCK * HEAD_DIM, CMP_HID).astype(BF16)
    w1p = w1.reshape(2, r, CMP_STRIDE, HEAD_DIM, CMP_HID).transpose(0, 2, 3, 1, 4)
    w1p = w1p.reshape(2, CMP_STRIDE, HEAD_DIM, r * CMP_HID)
    eye = jnp.eye(2, dtype=w1.dtype)
    w1_bd = w1p[:, :, None, :, None, :] * eye[None, None, :, None, :, None]
    w1_bd = w1_bd.reshape(2, CMP_STRIDE * 2 * HEAD_DIM, 2 * r * CMP_HID).astype(BF16)
    return (w_all, w_nat, w_kvt), pe_rows, w1_flat, w1_bd, b1.reshape(2, 1, CMP_HID), w2.astype(BF16)


def _nsa_prompt(x1, nsa_w, w_out, g, b):
    B, T, D = x1.shape
    (_, w_nat, w_kvt), pe_rows, w1_flat, w1_bd, b1, w2 = nsa_w
    att_w = N_KV * GROUP * HEAD_DIM
    kt_size = 256
    q, sz, gates, ksa, kwa, kvt, va = _nsa_inproj_prompt(x1, w_nat, w_kvt, d_q=att_w, kt_size=kt_size, tm=512)
    kvt6 = kvt.reshape(B, 6, N_KV, HEAD_DIM, T)
    rows = kvt6[:, :4].transpose(0, 4, 1, 2, 3)
    win = kvt6[:, 4:, :, :, T - min(WINDOW, T):].transpose(0, 4, 1, 2, 3)
    nchunk = T // CMP_STRIDE
    kcv = _sample_compress(kvt6, None, pe_rows, w1_flat, w1_bd, b1, w2, name="prompt_compress")
    mT = jnp.asarray(_cmp_to_sel_matrix(T // SEL_BLOCK, nchunk), BF16)
    o = _prompt_attn(q, gates, kcv[0], kcv[1].transpose(0, 1, 3, 2), ksa, kwa, va, mT, tq=128, kt_size=kt_size)
    y = _nsa_outproj(o.reshape(B * T, att_w), sz.reshape(B * T, att_w), x1.reshape(B * T, D), w_out, g, b, tm=512)
    return y.reshape(B, T, D), rows, win


def _nsa_sample(x1, cache_kv_l, cache_win_l, page_table, nsa_w, w_out, g, b):
    DB, T, D = x1.shape
    (w_all, _, _), pe_rows, w1_flat, w1_bd, b1, w2 = nsa_w
    att_w = N_KV * GROUP * HEAD_DIM
    kv_w = 6 * N_KV * HEAD_DIM
    page = cache_kv_l.shape[1]
    past = page_table.shape[1] * page
    assert T <= CMP_STRIDE and past % SEL_BLOCK == 0 and past % CMP_STRIDE == 0
    x_flat = x1.reshape(DB * T, D)
    q, kv, sz, gates = _nsa_inproj(x_flat, w_all, d_q=att_w, d_kv=kv_w, tm=DB * T)
    kv6 = kv.reshape(DB, T, 6, N_KV, HEAD_DIM)
    rows = kv6[:, :, :4]
    wseq = jnp.concatenate([cache_win_l, kv6[:, :, 4:]], axis=1)
    gd = N_KV * HEAD_DIM
    R = N_KV * GROUP * T
    qT = q.reshape(DB, T, N_KV, GROUP, HEAD_DIM).transpose(0, 2, 4, 3, 1)
    eye = jnp.eye(N_KV, dtype=BF16)
    qbd = (qT[:, :, :, None] * eye[None, :, None, :, None, None]).reshape(DB, gd, R)
    gT = gates[:, :3 * N_KV * GROUP].reshape(DB, T, 3, N_KV * GROUP).transpose(0, 2, 3, 1).reshape(DB, 3, R)
    gT = jnp.pad(gT, ((0, 0), (0, 5), (0, 0)))
    new_rows = lambda st: kv6[:, :, st].reshape(DB, T, gd).astype(BF16)
    cache_t = cache_kv_l.transpose(0, 2, 3, 4, 1)
    win_t = cache_win_l.transpose(0, 2, 3, 4, 1)
    wbuf = win_t.shape[-1]
    nchunk = past // CMP_STRIDE
    kcv = _sample_compress(cache_t, page_table, pe_rows, w1_flat, w1_bd, b1, w2)
    n_sel = past // SEL_BLOCK + 1
    n_sel_pad = -(-n_sel // (PAGES_PER_STEP * page // SEL_BLOCK)) * (PAGES_PER_STEP * page // SEL_BLOCK)
    mT = np.zeros((n_sel_pad, nchunk), np.float32)
    mT[:n_sel] = _cmp_to_sel_matrix(n_sel, nchunk)
    lane = np.arange(R)
    rr = ((lane[:, None] % T == lane[None, :] % T)
          & (lane[:, None] // (GROUP * T) == lane[None, :] // (GROUP * T))).astype(np.float32)
    oT = _sample_attn(cache_t, page_table, qbd, gT, kcv, new_rows(2), new_rows(3),
                      win_t[:, 0].reshape(DB, gd, wbuf), win_t[:, 1].reshape(DB, gd, wbuf), new_rows(4), new_rows(5),
                      jnp.asarray(mT, BF16), jnp.asarray(rr, BF16), t_new=T)
    o = oT.reshape(DB, HEAD_DIM, N_KV, GROUP, T).transpose(0, 4, 2, 3, 1).reshape(DB * T, att_w)
    y = _nsa_outproj(o, sz, x_flat, w_out, g, b, tm=DB * T)
    return y.reshape(DB, T, D), rows, wseq[:, T:]


def kernel(x_prompt, x_sample, state_conv, cache_kv, cache_win, page_table, conv_w_in, conv_b_in, conv_dw_w,
           conv_dw_b, conv_ln_g, conv_ln_b, conv_w_out, nsa_w_in, cmp_pe, cmp_w1, cmp_b1, cmp_w2, nsa_w_out,
           ln_g, ln_b):
    B, T, D = x_prompt.shape
    DB, TS, _ = x_sample.shape
    d_in = conv_w_out.shape[1]
    row = lambda v: v.reshape(1, -1)

    cw = (conv_w_in[0].astype(BF16), row(conv_b_in[0]), conv_dw_w[0], row(conv_dw_b[0]), row(conv_ln_g[0]),
          row(conv_ln_b[0]), conv_w_out[0].astype(BF16), row(ln_g[0]), row(ln_b[0]))
    pad = HALO - (CONV_W - 1)
    xp, sp = _conv_layer(x_prompt, jnp.zeros((B, HALO, d_in), F32), *cw, ts=256, carry=True)
    st = jnp.pad(state_conv[0], ((0, 0), (pad, 0), (0, 0)))
    xs, ss = _conv_layer(x_sample, st, *cw, ts=TS, carry=False)

    nsa_w = _prep_nsa_weights(nsa_w_in[0], cmp_pe[0], cmp_w1[0], cmp_b1[0], cmp_w2[0])
    w_out = nsa_w_out[0].astype(BF16)
    yp, rp, wp = _nsa_prompt(xp, nsa_w, w_out, row(ln_g[1]), row(ln_b[1]))
    ys, rs, ws = _nsa_sample(xs, cache_kv[0], cache_win[0], page_table, nsa_w, w_out, row(ln_g[1]), row(ln_b[1]))
    return (yp, ys, sp[None, :, pad:], ss[None, :, pad:], rp[None], rs[None], wp[None], ws[None])
```

```python
import functools

import jax
import jax.numpy as jnp
import numpy as np
from jax import lax
from jax.experimental import pallas as pl
from jax.experimental.pallas import tpu as pltpu

F32 = jnp.float32
BF16 = jnp.bfloat16

CONV_W = 31
N_KV = 4
GROUP = 4
HEAD_DIM = 64
CMP_BLOCK = 32
CMP_STRIDE = 16
CMP_HID = 128
SEL_BLOCK = 64
N_SELECT = 16
WINDOW = 512
DEPTH = 2
ALPHA = (2 * DEPTH) ** 0.25
LN_EPS = 1e-5
NEG = -1e30

VMEM_LIMIT = 56 * 1024 * 1024
LANES = 128
HALO = 32


def _cparams(sem):
    return pltpu.CompilerParams(dimension_semantics=sem, vmem_limit_bytes=VMEM_LIMIT)


def _sigmoid(x):
    return 1.0 / (1.0 + jnp.exp(-x))


def _layer_norm(x, g, b):
    mu = jnp.mean(x, axis=-1, keepdims=True)
    xc = x - mu
    var = jnp.mean(xc * xc, axis=-1, keepdims=True)
    return xc * lax.rsqrt(var + LN_EPS) * g + b


def _dot(a, b):
    return jnp.dot(a, b, preferred_element_type=F32)


def _conv_layer_kernel(x_ref, st_ref, win_ref, bin_ref, dww_ref, dwb_ref, clg_ref, clb_ref, wout_ref,
                       lng_ref, lnb_ref, y_ref, ns_ref, ubuf, zbuf, cbuf, *, ns, ts, d_in, carry):
    t = pl.program_id(1)
    rows = ns * ts
    cb = LANES
    n_cb = d_in // cb
    lanes = lambda ci: slice(ci * cb, (ci + 1) * cb)

    if carry:
        @pl.when(t == 0)
        def _():
            ubuf[:, :, 0:HALO, :] = jnp.zeros((ns, n_cb, HALO, cb), F32)
    else:
        for ci in range(n_cb):
            ubuf[:, ci, 0:HALO, :] = st_ref[:, :, lanes(ci)]

    x = x_ref[...].reshape(rows, x_ref.shape[-1])
    xb = x.astype(BF16)
    a = _dot(xb, win_ref[:, 0:d_in]) + bin_ref[:, 0:d_in]
    gl = _dot(xb, win_ref[:, d_in:2 * d_in]) + bin_ref[:, d_in:2 * d_in]
    u = (a * _sigmoid(gl)).reshape(ns, ts, d_in)
    for ci in range(n_cb):
        ubuf[:, ci, HALO:HALO + ts, :] = u[:, :, lanes(ci)]
    z = _dot(xb, win_ref[:, 2 * d_in:3 * d_in]) + bin_ref[:, 2 * d_in:3 * d_in]
    zbuf[...] = z * _sigmoid(z)

    rb = min(ts, 64)
    n_rb = ts // rb
    off0 = HALO - (CONV_W - 1)

    cpi = 2

    def chunk(i, c):
        s = i // (n_rb * n_cb // cpi)
        rem = i % (n_rb * n_cb // cpi)
        r0 = pl.multiple_of((rem // (n_cb // cpi)) * rb, rb)
        for e in range(cpi):
            ci = (rem % (n_cb // cpi)) * cpi + e
            c0 = pl.multiple_of(ci * cb, cb)
            acc = jnp.zeros((rb, cb), F32) + dwb_ref[:, pl.ds(c0, cb)]
            for k in range(CONV_W):
                acc = acc + dww_ref[k:k + 1, pl.ds(c0, cb)] * ubuf[s, ci, pl.ds(r0 + off0 + k, rb, stride=1), :]
            cbuf[s, pl.ds(r0, rb), pl.ds(c0, cb)] = acc
        return c

    lax.fori_loop(0, ns * n_rb * n_cb // cpi, chunk, 0)

    for ci in range(n_cb):
        ns_ref[:, :, lanes(ci)] = ubuf[:, ci, ts:ts + HALO, :]
    if carry:
        ubuf[:, :, 0:HALO, :] = ubuf[:, :, ts:ts + HALO, :]

    yc = _layer_norm(cbuf[...].reshape(rows, d_in), clg_ref[...], clb_ref[...])
    gated = (yc * _sigmoid(yc)) * zbuf[...]
    out = _dot(gated.astype(BF16), wout_ref[...])
    y = _layer_norm(ALPHA * x + out, lng_ref[...], lnb_ref[...])
    y_ref[...] = y.reshape(y_ref.shape)


def _conv_layer(x, state, w_in, b_in, dw_w, dw_b, cl_g, cl_b, w_out, ln_g, ln_b, *, ts, carry):
    S, T, D = x.shape
    d_in = w_out.shape[0]
    ns = 1 if carry else S
    grid = (S // ns, T // ts)
    const = lambda s, t: (0, 0)
    kernel = functools.partial(_conv_layer_kernel, ns=ns, ts=ts, d_in=d_in, carry=carry)
    y, new_state = pl.pallas_call(
        kernel,
        grid=grid,
        in_specs=[
            pl.BlockSpec((ns, ts, D), lambda s, t: (s, t, 0)),
            pl.BlockSpec((ns, HALO, d_in), lambda s, t: (s, 0, 0)),
            pl.BlockSpec(w_in.shape, const),
            pl.BlockSpec(b_in.shape, const),
            pl.BlockSpec(dw_w.shape, const),
            pl.BlockSpec(dw_b.shape, const),
            pl.BlockSpec(cl_g.shape, const),
            pl.BlockSpec(cl_b.shape, const),
            pl.BlockSpec(w_out.shape, const),
            pl.BlockSpec(ln_g.shape, const),
            pl.BlockSpec(ln_b.shape, const),
        ],
        out_specs=[
            pl.BlockSpec((ns, ts, D), lambda s, t: (s, t, 0)),
            pl.BlockSpec((ns, HALO, d_in), lambda s, t: (s, 0, 0)),
        ],
        out_shape=[
            jax.ShapeDtypeStruct((S, T, D), F32),
            jax.ShapeDtypeStruct((S, HALO, d_in), F32),
        ],
        scratch_shapes=[
            pltpu.VMEM((ns, d_in // LANES, HALO + ts, LANES), F32),
            pltpu.VMEM((ns * ts, d_in), F32),
            pltpu.VMEM((ns, ts, d_in), F32),
        ],
        compiler_params=_cparams(("arbitrary", "arbitrary")),
        name="conv_layer_carry" if carry else "conv_layer_state",
    )(x, state, w_in, b_in, dw_w, dw_b, cl_g, cl_b, w_out, ln_g, ln_b)
    return y, new_state


def _nsa_inproj_kernel(x_ref, w_ref, q_ref, kv_ref, sz_ref, g_ref, *, d_q, d_kv):
    xb = x_ref[...].astype(BF16)
    q_ref[...] = _dot(xb, w_ref[:, 0:d_q]).astype(BF16)
    kv_ref[...] = _dot(xb, w_ref[:, d_q:d_q + d_kv])
    z = _dot(xb, w_ref[:, d_q + d_kv:2 * d_q + d_kv])
    sz_ref[...] = z * _sigmoid(z)
    g_ref[...] = _sigmoid(_dot(xb, w_ref[:, 2 * d_q + d_kv:]))


def _nsa_inproj(x, w, *, d_q, d_kv, tm):
    N, D = x.shape
    d_g = w.shape[1] - 2 * d_q - d_kv
    row = lambda i: (i, 0)
    return pl.pallas_call(
        functools.partial(_nsa_inproj_kernel, d_q=d_q, d_kv=d_kv),
        grid=(N // tm,),
        in_specs=[pl.BlockSpec((tm, D), row), pl.BlockSpec(w.shape, lambda i: (0, 0))],
        out_specs=[pl.BlockSpec((tm, d_q), row), pl.BlockSpec((tm, d_kv), row),
                   pl.BlockSpec((tm, d_q), row), pl.BlockSpec((tm, d_g), row)],
        out_shape=[jax.ShapeDtypeStruct((N, d_q), BF16), jax.ShapeDtypeStruct((N, d_kv), F32),
                   jax.ShapeDtypeStruct((N, d_q), F32), jax.ShapeDtypeStruct((N, d_g), F32)],
        compiler_params=_cparams(("arbitrary",)),
        name="nsa_inproj",
    )(x, w)


def _nsa_inproj_prompt_kernel(x_ref, wn_ref, wt_ref, q_ref, sz_ref, g_ref, ksa_ref, kwa_ref, kvt_ref, va_ref,
                              *, d_q, kt_size):
    tm = x_ref.shape[1]
    t0 = pl.program_id(1) * tm
    xb = x_ref[0].astype(BF16)
    q_ref[0] = _dot(xb, wn_ref[:, 0:d_q]).astype(BF16)
    z = _dot(xb, wn_ref[:, d_q:2 * d_q])
    sz_ref[0] = z * _sigmoid(z)
    o = 2 * d_q
    g_ref[0] = _sigmoid(_dot(xb, wn_ref[:, o:o + LANES]))
    o += LANES
    kw = N_KV * K_AUG
    pos = t0 + lax.broadcasted_iota(jnp.int32, (tm, kw), 0)
    lane = lax.broadcasted_iota(jnp.int32, (tm, kw), 1) % K_AUG
    onehot = jnp.where(lane - HEAD_DIM == (pos % kt_size) // SEL_BLOCK, 1.0, 0.0)
    ksa_ref[0] = (_dot(xb, wn_ref[:, o:o + kw]) + onehot).astype(BF16)
    kwa_ref[0] = _dot(xb, wn_ref[:, o + kw:o + 2 * kw]).astype(BF16)
    hT = lax.dot_general(wt_ref[...], xb, (((1,), (1,)), ((), ())), preferred_element_type=F32)
    kvt_ref[0] = hT
    ones = jnp.concatenate([jnp.ones((1, tm), F32), jnp.zeros((V_AUG - HEAD_DIM - 1, tm), F32)], axis=0).astype(BF16)
    for i, st in enumerate((3, 5)):
        for g in range(N_KV):
            r0 = (st * N_KV + g) * HEAD_DIM
            va_ref[0, i, g, 0:HEAD_DIM, :] = hT[r0:r0 + HEAD_DIM].astype(BF16)
            va_ref[0, i, g, HEAD_DIM:V_AUG, :] = ones


def _nsa_inproj_prompt(x, w_nat, w_kvt, *, d_q, kt_size, tm):
    B, T, D = x.shape
    kw = N_KV * K_AUG
    tok = lambda n: pl.BlockSpec((1, tm, n), lambda b, t: (b, t, 0))
    const = lambda a: pl.BlockSpec(a.shape, lambda b, t: (0, 0))
    return pl.pallas_call(
        functools.partial(_nsa_inproj_prompt_kernel, d_q=d_q, kt_size=kt_size),
        grid=(B, T // tm),
        in_specs=[tok(D), const(w_nat), const(w_kvt)],
        out_specs=[tok(d_q), tok(d_q), tok(LANES), tok(kw), tok(kw),
                   pl.BlockSpec((1, w_kvt.shape[0], tm), lambda b, t: (b, 0, t)),
                   pl.BlockSpec((1, 2, N_KV, V_AUG, tm), lambda b, t: (b, 0, 0, 0, t))],
        out_shape=[jax.ShapeDtypeStruct((B, T, d_q), BF16), jax.ShapeDtypeStruct((B, T, d_q), F32),
                   jax.ShapeDtypeStruct((B, T, LANES), F32), jax.ShapeDtypeStruct((B, T, kw), BF16),
                   jax.ShapeDtypeStruct((B, T, kw), BF16), jax.ShapeDtypeStruct((B, w_kvt.shape[0], T), F32),
                   jax.ShapeDtypeStruct((B, 2, N_KV, V_AUG, T), BF16)],
        compiler_params=_cparams(("arbitrary", "arbitrary")),
        name="nsa_inproj_prompt",
    )(x, w_nat, w_kvt)


def _compress_finish(parts, base, w2):
    nchunk = parts.shape[0]
    second = pltpu.roll(parts[:, CMP_HID:], nchunk - 1, 0)
    row = lax.broadcasted_iota(jnp.int32, (nchunk, CMP_HID), 0)
    second = jnp.where(row < nchunk - 1, second, 0.0)
    h = base + parts[:, :CMP_HID] + second
    act = h * _sigmoid(h)
    return _dot(act.astype(BF16), w2).astype(BF16)


def _dot_f32_by_const(x, c, *, const_first):
    hi = x.astype(BF16)
    r1 = x - hi.astype(F32)
    mid = r1.astype(BF16)
    lo = (r1 - mid.astype(F32)).astype(BF16)
    out = None
    for part in (hi, mid, lo):
        d = _dot(c, part) if const_first else _dot(part, c)
        out = d if out is None else out + d
    return out


def _softmax_cols(s, valid):
    s = jnp.where(valid, s, NEG)
    m = jnp.max(s, axis=0, keepdims=True)
    e = jnp.where(valid, jnp.exp2(s - m), 0.0)
    return e / jnp.maximum(jnp.sum(e, axis=0, keepdims=True), 1e-30)


def _selection_scores(selimp, cur):
    j = lax.broadcasted_iota(jnp.int32, selimp.shape, 0)
    causal = j <= cur
    forced = (j == 0) | (j == cur) | (j == cur - 1)
    return jnp.where(forced, jnp.inf, jnp.where(causal, selimp, -jnp.inf)), causal


def _select_blocks(selimp, cur, n_select):
    ns = selimp.shape[0]
    j = lax.broadcasted_iota(jnp.int32, selimp.shape, 0)
    score, causal = _selection_scores(selimp, cur)
    taken = jnp.zeros(selimp.shape, jnp.int32)
    for _ in range(n_select):
        m = jnp.max(score, axis=0, keepdims=True)
        first = jnp.min(jnp.where(score == m, j, ns), axis=0, keepdims=True)
        hit = j == first
        taken = jnp.where(hit, 1, taken)
        score = jnp.where(hit, -jnp.inf, score)
    return (taken > 0) & causal


def _select_blocks_by_rank(selimp, cur, n_select, score_ref):
    ns = selimp.shape[0]
    score, causal = _selection_scores(selimp, cur)
    score_ref[...] = score
    sub = lax.broadcasted_iota(jnp.int32, (8, selimp.shape[1]), 0)
    ranks = []
    for v in range(ns // 8):
        blk = score[8 * v:8 * v + 8]
        rank = jnp.zeros(blk.shape, F32)
        for jp in range(ns):
            row = score_ref[jp:jp + 1, :]
            if jp < 8 * v:
                ahead = row >= blk
            elif jp >= 8 * v + 8:
                ahead = row > blk
            else:
                ahead = (row > blk) | ((row == blk) & (sub > jp - 8 * v))
            rank = rank + jnp.where(ahead, 1.0, 0.0)
        ranks.append(rank)
    return (jnp.concatenate(ranks, axis=0) < n_select) & causal


def _cmp_to_sel_matrix(n_sel, n_cmp):
    r = SEL_BLOCK // CMP_STRIDE
    lo = -(CMP_BLOCK // CMP_STRIDE - 1)
    m = np.zeros((n_sel, n_cmp), np.float32)
    for o in range(lo, r):
        start = o * CMP_STRIDE
        ov = max(0, min(SEL_BLOCK, start + CMP_BLOCK) - max(0, start))
        for jb in range(n_sel):
            n = r * jb + o
            if ov > 0 and 0 <= n < n_cmp:
                m[jb, n] += ov / CMP_BLOCK
    return m


V_AUG = 80
K_AUG = 128
BIAS_ROWS = 16


def _prompt_attn_kernel(q_ref, g_ref, kc_ref, vct_ref, ksa_ref, vsa_ref, kwa_ref, vwa_ref, mt_ref, o_ref,
                        bias_ref, score_ref, *, tq, kt_size, gps):
    qi = pl.program_id(1)
    qs = qi * tq
    hp = GROUP // 2
    RH = 2 * tq
    n_cmp = kc_ref.shape[2]
    n_sel = mt_ref.shape[0]
    blk_per_tile = kt_size // SEL_BLOCK
    chains = [(gi, h) for gi in range(gps) for h in range(hp)]

    t1 = qs + lax.broadcasted_iota(jnp.int32, (1, tq), 1)
    t_lane = jnp.concatenate([t1, t1], axis=1)
    pair = lambda gi, h: slice((gi * hp + h) * 2 * HEAD_DIM, (gi * hp + h + 1) * 2 * HEAD_DIM)
    qT = {}
    for gi, h in chains:
        t2 = q_ref[0, :, pair(gi, h)].astype(F32).T.astype(BF16)
        qT[gi, h] = jnp.concatenate([t2[0:HEAD_DIM], t2[HEAD_DIM:2 * HEAD_DIM]], axis=1)
    gT = g_ref[0].T
    k_lanes = lambda gi: slice(gi * K_AUG, (gi + 1) * K_AUG)

    wk = WINDOW + tq
    start = pl.multiple_of(jnp.maximum(qs - WINDOW, 0), 128)
    s_cmp = {c: _dot(kc_ref[0, c[0]], qT[c]) for c in chains}
    k_pad = jnp.zeros((K_AUG - HEAD_DIM, RH), BF16)
    s_win = {c: _dot(kwa_ref[0, pl.ds(start, wk), k_lanes(c[0])], jnp.concatenate([qT[c], k_pad], axis=0))
             for c in chains}

    n_idx = lax.broadcasted_iota(jnp.int32, (n_cmp, RH), 0)
    cmp_valid = n_idx * CMP_STRIDE + (CMP_BLOCK - 1) <= t_lane
    o_cmp = {}
    imps = [None] * gps
    for gi, h in chains:
        p_cmp = _softmax_cols(s_cmp[gi, h], cmp_valid)
        o_cmp[gi, h] = _dot(vct_ref[0, gi], p_cmp.astype(BF16))
        part = p_cmp[:, 0:tq] + p_cmp[:, tq:2 * tq]
        imps[gi] = part if imps[gi] is None else imps[gi] + part

    dpos = t_lane - (start + lax.broadcasted_iota(jnp.int32, (wk, RH), 0))
    win_valid = (dpos >= 0) & (dpos < WINDOW)
    o_win = {}
    for gi, h in chains:
        sw = jnp.where(win_valid, s_win[gi, h], NEG)
        p_win = jnp.exp2(sw - jnp.max(sw, axis=0, keepdims=True)).astype(BF16)
        acc_win = _dot(vwa_ref[0, 0, gi, :, pl.ds(start, wk)], p_win)
        o_win[gi, h] = acc_win[0:HEAD_DIM] / jnp.maximum(acc_win[HEAD_DIM:HEAD_DIM + 1], 1e-30)

    pad = jnp.zeros((BIAS_ROWS - blk_per_tile, tq), F32)
    for gi in range(gps):
        selimp = _dot_f32_by_const(imps[gi], mt_ref[...], const_first=True)
        sel = _select_blocks_by_rank(selimp, t1 // SEL_BLOCK, N_SELECT, score_ref.at[gi])
        bias = jnp.where(sel, 0.0, NEG)
        rows = []
        for kt in range(n_sel // blk_per_tile):
            rows += [bias[kt * blk_per_tile:(kt + 1) * blk_per_tile], pad]
        bias_rows = jnp.concatenate(rows, axis=0)
        bias_ref[gi] = jnp.concatenate([bias_rows, bias_rows], axis=1).astype(BF16)

    q_pad = jnp.zeros((K_AUG - HEAD_DIM - BIAS_ROWS, RH), BF16)

    def sel_units(kt, carries, diagonal):
        k0 = pl.multiple_of(kt * kt_size, kt_size)
        scores = []
        for gi, h in chains:
            b_t = bias_ref[gi, pl.ds(pl.multiple_of(kt * BIAS_ROWS, BIAS_ROWS), BIAS_ROWS), :]
            qa = jnp.concatenate([qT[gi, h], b_t, q_pad], axis=0)
            scores.append(_dot(ksa_ref[0, pl.ds(k0, kt_size), k_lanes(gi)], qa))
        out = []
        for (gi, h), s, (m, acc) in zip(chains, scores, carries):
            if diagonal:
                kidx = k0 + lax.broadcasted_iota(jnp.int32, (kt_size, RH), 0)
                s = jnp.where(kidx <= t_lane, s, NEG)
            m_new = jnp.maximum(m, jnp.max(s, axis=0, keepdims=True))
            p = jnp.exp2(s - m_new).astype(BF16)
            acc = jnp.exp2(m - m_new) * acc + _dot(vsa_ref[0, 0, gi, :, pl.ds(k0, kt_size)], p)
            out.append((m_new, acc))
        return tuple(out)

    n_full = qs // kt_size
    init = tuple((jnp.full((1, RH), NEG, F32), jnp.zeros((V_AUG, RH), F32)) for _ in chains)
    carries = lax.fori_loop(0, n_full, lambda kt, c: sel_units(kt, c, False), init)
    carries = sel_units(n_full, carries, True)

    def gate(gi, h, branch):
        r0 = branch * N_KV * GROUP + gi * GROUP + 2 * h
        return jnp.concatenate([gT[r0:r0 + 1], gT[r0 + 1:r0 + 2]], axis=1)

    for i, (gi, h) in enumerate(chains):
        acc_sel = carries[i][1]
        o_sel = acc_sel[0:HEAD_DIM] / jnp.maximum(acc_sel[HEAD_DIM:HEAD_DIM + 1], 1e-30)
        o = gate(gi, h, 0) * o_cmp[gi, h] + gate(gi, h, 1) * o_sel + gate(gi, h, 2) * o_win[gi, h]
        o_ref[0, :, pair(gi, h)] = jnp.concatenate([o[:, 0:tq], o[:, tq:2 * tq]], axis=0).T


def _prompt_attn(q, gates, kc, vcT, ksa, kwa, va, mT, *, tq, kt_size):
    B, T, att_w = q.shape
    G, NC, Dh = kc.shape[1:]
    n_sel = mT.shape[0]
    assert kt_size % SEL_BLOCK == 0 and kt_size // SEL_BLOCK <= BIAS_ROWS and T % kt_size == 0 and kt_size % tq == 0
    assert tq == LANES and G == N_KV
    whole = lambda a: pl.BlockSpec((1,) + a.shape[1:], lambda b, i: (b,) + (0,) * (a.ndim - 1))
    v_spec = lambda s: pl.BlockSpec((1, 1, G, V_AUG, T), lambda b, i: (b, s, 0, 0, 0))
    return pl.pallas_call(
        functools.partial(_prompt_attn_kernel, tq=tq, kt_size=kt_size, gps=G),
        grid=(B, T // tq),
        in_specs=[
            pl.BlockSpec((1, tq, att_w), lambda b, i: (b, i, 0)),
            pl.BlockSpec((1, tq, gates.shape[2]), lambda b, i: (b, i, 0)),
            whole(kc), whole(vcT), whole(ksa), v_spec(0), whole(kwa), v_spec(1),
            pl.BlockSpec(mT.shape, lambda b, i: (0, 0)),
        ],
        out_specs=pl.BlockSpec((1, tq, att_w), lambda b, i: (b, i, 0)),
        out_shape=jax.ShapeDtypeStruct((B, T, att_w), F32),
        scratch_shapes=[pltpu.VMEM((G, T // kt_size * BIAS_ROWS, 2 * tq), BF16),
                        pltpu.VMEM((G, n_sel, tq), F32)],
        compiler_params=_cparams(("arbitrary", "arbitrary")),
        name="prompt_attn",
    )(q, gates, kc, vcT, ksa, va, kwa, va, mT)


def _dot_tn(v, p):
    return lax.dot_general(v, p, (((0,), (0,)), ((), ())), preferred_element_type=F32)


PAGES_PER_STEP = 16
PANEL_PITCH = CMP_STRIDE + 1


def _page_specs(block_streams, stream_block, page):
    def spec(j):
        return pl.BlockSpec((1, block_streams, N_KV, HEAD_DIM, page),
                            lambda b, kt, pt: (pt[b, kt * PAGES_PER_STEP + j], stream_block, 0, 0, 0))
    return spec


def _sample_compress_kernel(pt_ref, *refs, page):
    del pt_ref
    pages = refs[:PAGES_PER_STEP]
    pe_ref, w1_ref, wp_ref, b1_ref, w2_ref, o_ref, panel, xs = refs[PAGES_PER_STEP:]
    kt = pl.program_id(1)
    cps = (page // CMP_STRIDE) * PAGES_PER_STEP
    row0 = pl.multiple_of(kt * cps, cps)
    for st in range(2):
        for gp in range(N_KV // 2):
            for j, p_ref in enumerate(pages):
                tile = p_ref[0, st, 2 * gp:2 * gp + 2].reshape(2 * HEAD_DIM, page).T
                for c in range(page // CMP_STRIDE):
                    r0 = (j * (page // CMP_STRIDE) + c) * PANEL_PITCH
                    panel[pl.ds(r0, CMP_STRIDE, stride=1), :] = tile[c * CMP_STRIDE:(c + 1) * CMP_STRIDE]
            for s in range(CMP_STRIDE):
                xs[st, gp, s, pl.ds(row0, cps), :] = panel[pl.ds(s, cps, stride=PANEL_PITCH), :].astype(BF16)

    @pl.when(kt == pl.num_programs(1) - 1)
    def _():
        for st in range(2):
            base = _dot(pe_ref[st], w1_ref[st])[0:1, :] + b1_ref[st]
            for gp in range(N_KV // 2):
                lhs = jnp.concatenate([xs[st, gp, s] for s in range(CMP_STRIDE)], axis=1)
                acc = _dot(lhs, wp_ref[st])
                for g2 in range(2):
                    parts = acc[:, g2 * 2 * CMP_HID:(g2 + 1) * 2 * CMP_HID]
                    o_ref[st, 0, 2 * gp + g2] = _compress_finish(parts, base, w2_ref[st])


def _sample_compress(cache_t, page_table, pe, w1, wp, b1, w2, *, name="sample_compress"):
    page = 2 * HEAD_DIM
    if page_table is None:
        DB, n_pages = cache_t.shape[0], cache_t.shape[-1] // page
        page_table = jnp.zeros((1, 1), jnp.int32)

        def spec(j):
            return pl.BlockSpec((1, 2, N_KV, HEAD_DIM, page), lambda b, kt, pt: (b, 0, 0, 0, kt * PAGES_PER_STEP + j))
    else:
        DB, n_pages = page_table.shape
        assert cache_t.shape[-1] == page
        spec = _page_specs(2, 0, page)
    nchunk = n_pages * page // CMP_STRIDE
    assert n_pages % PAGES_PER_STEP == 0 and page == LANES
    const = lambda a: pl.BlockSpec(a.shape, lambda b, kt, pt: (0,) * a.ndim)
    return pl.pallas_call(
        functools.partial(_sample_compress_kernel, page=page),
        grid_spec=pltpu.PrefetchScalarGridSpec(
            num_scalar_prefetch=1,
            grid=(DB, n_pages // PAGES_PER_STEP),
            in_specs=[spec(j) for j in range(PAGES_PER_STEP)] + [const(pe), const(w1), const(wp), const(b1), const(w2)],
            out_specs=pl.BlockSpec((2, 1, N_KV, nchunk, HEAD_DIM), lambda b, kt, pt: (0, b, 0, 0, 0)),
            scratch_shapes=[pltpu.VMEM((PAGES_PER_STEP * page // CMP_STRIDE * PANEL_PITCH, 2 * HEAD_DIM), F32),
                            pltpu.VMEM((2, N_KV // 2, CMP_STRIDE, nchunk, 2 * HEAD_DIM), BF16)],
        ),
        out_shape=jax.ShapeDtypeStruct((2, DB, N_KV, nchunk, HEAD_DIM), BF16),
        compiler_params=_cparams(("arbitrary", "arbitrary")),
        name=name,
    )(page_table, *([cache_t] * PAGES_PER_STEP), pe, w1, wp, b1, w2)


def _sample_attn_kernel(pt_ref, *refs, t_new, past, page):
    del pt_ref
    pages = refs[:PAGES_PER_STEP]
    (q_ref, g_ref, kc_ref, vc_ref, kn_ref, vn_ref, kw_ref, vw_ref, kwn_ref, vwn_ref, mt_ref, rr_ref,
     o_ref, bias_ref, m_ref, l_ref, acc_ref, oc_ref) = refs[PAGES_PER_STEP:]
    kt = pl.program_id(1)
    gd = N_KV * HEAD_DIM
    R = N_KV * GROUP * t_new
    qbd = q_ref[0]
    lane = lax.broadcasted_iota(jnp.int32, (1, R), 1)
    i_lane = lane % t_new
    t_lane = past + i_lane

    @pl.when(kt == 0)
    def _():
        n_cmp = kc_ref.shape[2]
        sc = None
        for g in range(N_KV):
            d = _dot(kc_ref[0, g], qbd[g * HEAD_DIM:(g + 1) * HEAD_DIM, :])
            sc = d if sc is None else sc + d
        n_idx = lax.broadcasted_iota(jnp.int32, (n_cmp, R), 0)
        p_cmp = _softmax_cols(sc, n_idx * CMP_STRIDE + (CMP_BLOCK - 1) <= t_lane)
        pb = p_cmp.astype(BF16)
        o_cmp = jnp.concatenate([_dot_tn(vc_ref[0, g], pb) for g in range(N_KV)], axis=0)

        imp = _dot_f32_by_const(p_cmp, rr_ref[...], const_first=False)
        selimp = _dot_f32_by_const(imp, mt_ref[...], const_first=True)
        sel = _select_blocks(selimp, t_lane // SEL_BLOCK, N_SELECT)
        bias_ref[...] = jnp.where(sel, 0.0, NEG)

        wbuf = kw_ref.shape[-1]
        sw = jnp.concatenate([_dot_tn(kw_ref[0].astype(BF16), qbd), _dot(kwn_ref[0], qbd)], axis=0)
        jw = lax.broadcasted_iota(jnp.int32, (wbuf + t_new, R), 0)
        dpos = wbuf + i_lane - jw
        p_win = _softmax_cols(sw, (dpos >= 0) & (dpos < WINDOW) & (past - wbuf + jw >= 0)).astype(BF16)
        o_win = _dot(vw_ref[0].astype(BF16), p_win[0:wbuf]) + _dot_tn(vwn_ref[0], p_win[wbuf:wbuf + t_new])
        oc_ref[...] = g_ref[0, 0:1, :] * o_cmp + g_ref[0, 2:3, :] * o_win

        new_blk = past // SEL_BLOCK
        s_new = _dot(kn_ref[0], qbd) + bias_ref[new_blk:new_blk + 1, :]
        ip = lax.broadcasted_iota(jnp.int32, (t_new, R), 0)
        s_new = jnp.where(ip <= i_lane, s_new, NEG)
        m0 = jnp.max(s_new, axis=0, keepdims=True)
        p0 = jnp.exp2(s_new - m0)
        m_ref[...] = m0
        l_ref[...] = jnp.sum(p0, axis=0, keepdims=True)
        acc_ref[...] = _dot_tn(vn_ref[0], p0.astype(BF16))

    blk_per_page = page // SEL_BLOCK
    nb = PAGES_PER_STEP * blk_per_page
    b_t = bias_ref[pl.ds(pl.multiple_of(kt * nb, nb), nb), :]
    kT = jnp.concatenate([p_ref[0, 0].reshape(gd, page).astype(BF16) for p_ref in pages], axis=1)
    vT = jnp.concatenate([p_ref[0, 1].reshape(gd, page).astype(BF16) for p_ref in pages], axis=1)
    half = kT.shape[1] // 2
    s = jnp.concatenate([_dot_tn(kT[:, 0:half], qbd), _dot_tn(kT[:, half:], qbd)], axis=0)
    s = jnp.concatenate([s[jb * SEL_BLOCK:(jb + 1) * SEL_BLOCK] + b_t[jb:jb + 1, :] for jb in range(nb)], axis=0)
    m_old = m_ref[...]
    m_new = jnp.maximum(m_old, jnp.max(s, axis=0, keepdims=True))
    alpha = jnp.exp2(m_old - m_new)
    p = jnp.exp2(s - m_new)
    l = alpha * l_ref[...] + jnp.sum(p, axis=0, keepdims=True)
    pb = p.astype(BF16)
    acc = alpha * acc_ref[...] + jnp.concatenate([_dot(vT[0:gd // 2], pb), _dot(vT[gd // 2:], pb)], axis=0)
    m_ref[...] = m_new
    l_ref[...] = l
    acc_ref[...] = acc

    @pl.when(kt == pl.num_programs(1) - 1)
    def _():
        o = oc_ref[...] + g_ref[0, 1:2, :] * (acc / jnp.maximum(l, 1e-30))
        grp = lane // (GROUP * t_new)
        out = jnp.zeros((HEAD_DIM, R), F32)
        for g in range(N_KV):
            out = out + jnp.where(grp == g, o[g * HEAD_DIM:(g + 1) * HEAD_DIM, :], 0.0)
        o_ref[0] = out


def _sample_attn(cache_t, page_table, qbd, gT, kcv, kn, vn, kw, vw, kwn, vwn, mT, rr, *, t_new):
    page = cache_t.shape[-1]
    DB, n_pages = page_table.shape
    past = n_pages * page
    R = qbd.shape[-1]
    per_b = lambda a: pl.BlockSpec((1,) + a.shape[1:], lambda b, kt, pt: (b,) + (0,) * (a.ndim - 1))
    const = lambda a: pl.BlockSpec(a.shape, lambda b, kt, pt: (0,) * a.ndim)
    cmp_spec = lambda st: pl.BlockSpec((None, 1) + kcv.shape[2:], lambda b, kt, pt: (st, b, 0, 0, 0))
    spec = _page_specs(2, 1, page)
    return pl.pallas_call(
        functools.partial(_sample_attn_kernel, t_new=t_new, past=past, page=page),
        grid_spec=pltpu.PrefetchScalarGridSpec(
            num_scalar_prefetch=1,
            grid=(DB, n_pages // PAGES_PER_STEP),
            in_specs=[spec(j) for j in range(PAGES_PER_STEP)]
            + [per_b(qbd), per_b(gT), cmp_spec(0), cmp_spec(1), per_b(kn), per_b(vn), per_b(kw), per_b(vw),
               per_b(kwn), per_b(vwn), const(mT), const(rr)],
            out_specs=pl.BlockSpec((1, HEAD_DIM, R), lambda b, kt, pt: (b, 0, 0)),
            scratch_shapes=[pltpu.VMEM((mT.shape[0], R), F32), pltpu.VMEM((1, R), F32), pltpu.VMEM((1, R), F32),
                            pltpu.VMEM((N_KV * HEAD_DIM, R), F32), pltpu.VMEM((N_KV * HEAD_DIM, R), F32)],
        ),
        out_shape=jax.ShapeDtypeStruct((DB, HEAD_DIM, R), F32),
        compiler_params=_cparams(("arbitrary", "arbitrary")),
        name="sample_attn",
    )(page_table, *([cache_t] * PAGES_PER_STEP), qbd, gT, kcv, kcv, kn, vn, kw, vw, kwn, vwn, mT, rr)


def _nsa_outproj_kernel(o_ref, sz_ref, x_ref, w_ref, g_ref, b_ref, y_ref):
    gated = (o_ref[...] * sz_ref[...]).astype(BF16)
    y = _dot(gated, w_ref[...])
    y_ref[...] = _layer_norm(ALPHA * x_ref[...] + y, g_ref[...], b_ref[...])


def _nsa_outproj(o, sz, x, w, g, b, *, tm):
    N, D = x.shape
    row = lambda i: (i, 0)
    const = lambda i: (0, 0)
    return pl.pallas_call(
        _nsa_outproj_kernel,
        grid=(N // tm,),
        in_specs=[pl.BlockSpec((tm, o.shape[1]), row), pl.BlockSpec((tm, sz.shape[1]), row),
                  pl.BlockSpec((tm, D), row), pl.BlockSpec(w.shape, const),
                  pl.BlockSpec(g.shape, const), pl.BlockSpec(b.shape, const)],
        out_specs=pl.BlockSpec((tm, D), row),
        out_shape=jax.ShapeDtypeStruct((N, D), F32),
        compiler_params=_cparams(("arbitrary",)),
        name="nsa_outproj",
    )(o, sz, x, w, g, b)


def _prep_nsa_weights(w_in, pe, w1, b1, w2):
    D = w_in.shape[0]
    att_w = N_KV * GROUP * HEAD_DIM
    kv_w = 6 * N_KV * HEAD_DIM
    o1, o2 = att_w, att_w + kv_w
    o3 = o2 + 3 * N_KV * GROUP
    w_q = w_in[:, :o1] * (HEAD_DIM ** -0.5 * np.log2(np.e))
    w_g = w_in[:, o2:o3].reshape(D, N_KV, GROUP, 3).transpose(0, 3, 1, 2).reshape(D, 3 * N_KV * GROUP)
    w_g = jnp.pad(w_g, ((0, 0), (0, 128 - w_g.shape[1])))
    w_all = jnp.concatenate([w_q, w_in[:, o1:o2], w_in[:, o3:], w_g], axis=1).astype(BF16)
    kv_cols = w_in[:, o1:o2].reshape(D, 6, N_KV, HEAD_DIM)
    padded = lambda st: jnp.pad(kv_cols[:, st], ((0, 0), (0, 0), (0, K_AUG - HEAD_DIM))).reshape(D, N_KV * K_AUG)
    w_nat = jnp.concatenate([w_q, w_in[:, o3:], w_g, padded(2), padded(4)], axis=1).astype(BF16)
    w_kvt = w_in[:, o1:o2].T.astype(BF16)
    r = CMP_BLOCK // CMP_STRIDE
    pe_rows = jnp.pad(pe.reshape(2, 1, CMP_BLOCK * HEAD_DIM), ((0, 0), (0, 7), (0, 0))).astype(BF16)
    w1_flat = w1.reshape(2, CMP_BLOCK * HEAD_DIM, CMP_HID).astype(BF16)
    w1p = w1.reshape(2, r, CMP_STRIDE, HEAD_DIM, CMP_HID).transpose(0, 2, 3, 1, 4)
    w1p = w1p.reshape(2, CMP_STRIDE, HEAD_DIM, r * CMP_HID)
    eye = jnp.eye(2, dtype=w1.dtype)
    w1_bd = w1p[:, :, None, :, None, :] * eye[None, None, :, None, :, None]
    w1_bd = w1_bd.reshape(2, CMP_STRIDE * 2 * HEAD_DIM, 2 * r * CMP_HID).astype(BF16)
    return (w_all, w_nat, w_kvt), pe_rows, w1_flat, w1_bd, b1.reshape(2, 1, CMP_HID), w2.astype(BF16)


def _nsa_prompt(x1, nsa_w, w_out, g, b):
    B, T, D = x1.shape
    (_, w_nat, w_kvt), pe_rows, w1_flat, w1_bd, b1, w2 = nsa_w
    att_w = N_KV * GROUP * HEAD_DIM
    kt_size = 256
    q, sz, gates, ksa, kwa, kvt, va = _nsa_inproj_prompt(x1, w_nat, w_kvt, d_q=att_w, kt_size=kt_size, tm=512)
    kvt6 = kvt.reshape(B, 6, N_KV, HEAD_DIM, T)
    rows = kvt6[:, :4].transpose(0, 4, 1, 2, 3)
    win = kvt6[:, 4:, :, :, T - min(WINDOW, T):].transpose(0, 4, 1, 2, 3)
    nchunk = T // CMP_STRIDE
    kcv = _sample_compress(kvt6, None, pe_rows, w1_flat, w1_bd, b1, w2, name="prompt_compress")
    mT = jnp.asarray(_cmp_to_sel_matrix(T // SEL_BLOCK, nchunk), BF16)
    o = _prompt_attn(q, gates, kcv[0], kcv[1].transpose(0, 1, 3, 2), ksa, kwa, va, mT, tq=128, kt_size=kt_size)
    y = _nsa_outproj(o.reshape(B * T, att_w), sz.reshape(B * T, att_w), x1.reshape(B * T, D), w_out, g, b, tm=512)
    return y.reshape(B, T, D), rows, win


def _nsa_sample(x1, cache_kv_l, cache_win_l, page_table, nsa_w, w_out, g, b):
    DB, T, D = x1.shape
    (w_all, _, _), pe_rows, w1_flat, w1_bd, b1, w2 = nsa_w
    att_w = N_KV * GROUP * HEAD_DIM
    kv_w = 6 * N_KV * HEAD_DIM
    page = cache_kv_l.shape[1]
    past = page_table.shape[1] * page
    assert T <= CMP_STRIDE and past % SEL_BLOCK == 0 and past % CMP_STRIDE == 0
    x_flat = x1.reshape(DB * T, D)
    q, kv, sz, gates = _nsa_inproj(x_flat, w_all, d_q=att_w, d_kv=kv_w, tm=DB * T)
    kv6 = kv.reshape(DB, T, 6, N_KV, HEAD_DIM)
    rows = kv6[:, :, :4]
    wseq = jnp.concatenate([cache_win_l, kv6[:, :, 4:]], axis=1)
    gd = N_KV * HEAD_DIM
    R = N_KV * GROUP * T
    qT = q.reshape(DB, T, N_KV, GROUP, HEAD_DIM).transpose(0, 2, 4, 3, 1)
    eye = jnp.eye(N_KV, dtype=BF16)
    qbd = (qT[:, :, :, None] * eye[None, :, None, :, None, None]).reshape(DB, gd, R)
    gT = gates[:, :3 * N_KV * GROUP].reshape(DB, T, 3, N_KV * GROUP).transpose(0, 2, 3, 1).reshape(DB, 3, R)
    gT = jnp.pad(gT, ((0, 0), (0, 5), (0, 0)))
    new_rows = lambda st: kv6[:, :, st].reshape(DB, T, gd).astype(BF16)
    cache_t = cache_kv_l.transpose(0, 2, 3, 4, 1)
    win_t = cache_win_l.transpose(0, 2, 3, 4, 1)
    wbuf = win_t.shape[-1]
    nchunk = past // CMP_STRIDE
    kcv = _sample_compress(cache_t, page_table, pe_rows, w1_flat, w1_bd, b1, w2)
    n_sel = past // SEL_BLOCK + 1
    n_sel_pad = -(-n_sel // (PAGES_PER_STEP * page // SEL_BLOCK)) * (PAGES_PER_STEP * page // SEL_BLOCK)
    mT = np.zeros((n_sel_pad, nchunk), np.float32)
    mT[:n_sel] = _cmp_to_sel_matrix(n_sel, nchunk)
    lane = np.arange(R)
    rr = ((lane[:, None] % T == lane[None, :] % T)
          & (lane[:, None] // (GROUP * T) == lane[None, :] // (GROUP * T))).astype(np.float32)
    oT = _sample_attn(cache_t, page_table, qbd, gT, kcv, new_rows(2), new_rows(3),
                      win_t[:, 0].reshape(DB, gd, wbuf), win_t[:, 1].reshape(DB, gd, wbuf), new_rows(4), new_rows(5),
                      jnp.asarray(mT, BF16), jnp.asarray(rr, BF16), t_new=T)
    o = oT.reshape(DB, HEAD_DIM, N_KV, GROUP, T).transpose(0, 4, 2, 3, 1).reshape(DB * T, att_w)
    y = _nsa_outproj(o, sz, x_flat, w_out, g, b, tm=DB * T)
    return y.reshape(DB, T, D), rows, wseq[:, T:]


def kernel(x_prompt, x_sample, state_conv, cache_kv, cache_win, page_table, conv_w_in, conv_b_in, conv_dw_w,
           conv_dw_b, conv_ln_g, conv_ln_b, conv_w_out, nsa_w_in, cmp_pe, cmp_w1, cmp_b1, cmp_w2, nsa_w_out,
           ln_g, ln_b):
    B, T, D = x_prompt.shape
    DB, TS, _ = x_sample.shape
    d_in = conv_w_out.shape[1]
    row = lambda v: v.reshape(1, -1)

    cw = (conv_w_in[0].astype(BF16), row(conv_b_in[0]), conv_dw_w[0], row(conv_dw_b[0]), row(conv_ln_g[0]),
          row(conv_ln_b[0]), conv_w_out[0].astype(BF16), row(ln_g[0]), row(ln_b[0]))
    pad = HALO - (CONV_W - 1)
    xp, sp = _conv_layer(x_prompt, jnp.zeros((B, HALO, d_in), F32), *cw, ts=256, carry=True)
    st = jnp.pad(state_conv[0], ((0, 0), (pad, 0), (0, 0)))
    xs, ss = _conv_layer(x_sample, st, *cw, ts=TS, carry=False)

    nsa_w = _prep_nsa_weights(nsa_w_in[0], cmp_pe[0], cmp_w1[0], cmp_b1[0], cmp_w2[0])
    w_out = nsa_w_out[0].astype(BF16)
    yp, rp, wp = _nsa_prompt(xp, nsa_w, w_out, row(ln_g[1]), row(ln_b[1]))
    ys, rs, ws = _nsa_sample(xs, cache_kv[0], cache_win[0], page_table, nsa_w, w_out, row(ln_g[1]), row(ln_b[1]))
    return (yp, ys, sp[None, :, pad:], ss[None, :, pad:], rp[None], rs[None], wp[None], ws[None])
```

```python
import functools

import jax
import jax.numpy as jnp
import numpy as np
from jax import lax
from jax.experimental import pallas as pl
from jax.experimental.pallas import tpu as pltpu

F32 = jnp.float32
BF16 = jnp.bfloat16

CONV_W = 31
N_KV = 4
GROUP = 4
HEAD_DIM = 64
CMP_BLOCK = 32
CMP_STRIDE = 16
CMP_HID = 128
SEL_BLOCK = 64
N_SELECT = 16
WINDOW = 512
DEPTH = 2
ALPHA = (2 * DEPTH) ** 0.25
LN_EPS = 1e-5
NEG = -1e30

VMEM_LIMIT = 56 * 1024 * 1024
LANES = 128
HALO = 32


def _cparams(sem):
    return pltpu.CompilerParams(dimension_semantics=sem, vmem_limit_bytes=VMEM_LIMIT)


def _sigmoid(x):
    return 1.0 / (1.0 + jnp.exp(-x))


def _layer_norm(x, g, b):
    mu = jnp.mean(x, axis=-1, keepdims=True)
    xc = x - mu
    var = jnp.mean(xc * xc, axis=-1, keepdims=True)
    return xc * lax.rsqrt(var + LN_EPS) * g + b


def _dot(a, b):
    return jnp.dot(a, b, preferred_element_type=F32)


def _conv_layer_kernel(x_ref, st_ref, win_ref, bin_ref, dww_ref, dwb_ref, clg_ref, clb_ref, wout_ref,
                       lng_ref, lnb_ref, y_ref, ns_ref, ubuf, zbuf, cbuf, *, ns, ts, d_in, carry):
    t = pl.program_id(1)
    rows = ns * ts
    cb = LANES
    n_cb = d_in // cb
    lanes = lambda ci: slice(ci * cb, (ci + 1) * cb)

    if carry:
        @pl.when(t == 0)
        def _():
            ubuf[:, :, 0:HALO, :] = jnp.zeros((ns, n_cb, HALO, cb), F32)
    else:
        for ci in range(n_cb):
            ubuf[:, ci, 0:HALO, :] = st_ref[:, :, lanes(ci)]

    x = x_ref[...].reshape(rows, x_ref.shape[-1])
    xb = x.astype(BF16)
    a = _dot(xb, win_ref[:, 0:d_in]) + bin_ref[:, 0:d_in]
    gl = _dot(xb, win_ref[:, d_in:2 * d_in]) + bin_ref[:, d_in:2 * d_in]
    u = (a * _sigmoid(gl)).reshape(ns, ts, d_in)
    for ci in range(n_cb):
        ubuf[:, ci, HALO:HALO + ts, :] = u[:, :, lanes(ci)]
    z = _dot(xb, win_ref[:, 2 * d_in:3 * d_in]) + bin_ref[:, 2 * d_in:3 * d_in]
    zbuf[...] = z * _sigmoid(z)

    rb = min(ts, 32)
    n_rb = ts // rb
    off0 = HALO - (CONV_W - 1)

    cpi = 4

    def chunk(i, c):
        s = i // (n_rb * n_cb // cpi)
        rem = i % (n_rb * n_cb // cpi)
        r0 = pl.multiple_of((rem // (n_cb // cpi)) * rb, rb)
        for e in range(cpi):
            ci = (rem % (n_cb // cpi)) * cpi + e
            c0 = pl.multiple_of(ci * cb, cb)
            acc = jnp.zeros((rb, cb), F32) + dwb_ref[:, pl.ds(c0, cb)]
            for k in range(CONV_W):
                acc = acc + dww_ref[k:k + 1, pl.ds(c0, cb)] * ubuf[s, ci, pl.ds(r0 + off0 + k, rb, stride=1), :]
            cbuf[s, pl.ds(r0, rb), pl.ds(c0, cb)] = acc
        return c

    lax.fori_loop(0, ns * n_rb * n_cb // cpi, chunk, 0)

    for ci in range(n_cb):
        ns_ref[:, :, lanes(ci)] = ubuf[:, ci, ts:ts + HALO, :]
    if carry:
        ubuf[:, :, 0:HALO, :] = ubuf[:, :, ts:ts + HALO, :]

    yc = _layer_norm(cbuf[...].reshape(rows, d_in), clg_ref[...], clb_ref[...])
    gated = (yc * _sigmoid(yc)) * zbuf[...]
    out = _dot(gated.astype(BF16), wout_ref[...])
    y = _layer_norm(ALPHA * x + out, lng_ref[...], lnb_ref[...])
    y_ref[...] = y.reshape(y_ref.shape)


def _conv_layer(x, state, w_in, b_in, dw_w, dw_b, cl_g, cl_b, w_out, ln_g, ln_b, *, ts, carry):
    S, T, D = x.shape
    d_in = w_out.shape[0]
    ns = 1 if carry else S
    grid = (S // ns, T // ts)
    const = lambda s, t: (0, 0)
    kernel = functools.partial(_conv_layer_kernel, ns=ns, ts=ts, d_in=d_in, carry=carry)
    y, new_state = pl.pallas_call(
        kernel,
        grid=grid,
        in_specs=[
            pl.BlockSpec((ns, ts, D), lambda s, t: (s, t, 0)),
            pl.BlockSpec((ns, HALO, d_in), lambda s, t: (s, 0, 0)),
            pl.BlockSpec(w_in.shape, const),
            pl.BlockSpec(b_in.shape, const),
            pl.BlockSpec(dw_w.shape, const),
            pl.BlockSpec(dw_b.shape, const),
            pl.BlockSpec(cl_g.shape, const),
            pl.BlockSpec(cl_b.shape, const),
            pl.BlockSpec(w_out.shape, const),
            pl.BlockSpec(ln_g.shape, const),
            pl.BlockSpec(ln_b.shape, const),
        ],
        out_specs=[
            pl.BlockSpec((ns, ts, D), lambda s, t: (s, t, 0)),
            pl.BlockSpec((ns, HALO, d_in), lambda s, t: (s, 0, 0)),
        ],
        out_shape=[
            jax.ShapeDtypeStruct((S, T, D), F32),
            jax.ShapeDtypeStruct((S, HALO, d_in), F32),
        ],
        scratch_shapes=[
            pltpu.VMEM((ns, d_in // LANES, HALO + ts, LANES), F32),
            pltpu.VMEM((ns * ts, d_in), F32),
            pltpu.VMEM((ns, ts, d_in), F32),
        ],
        compiler_params=_cparams(("arbitrary", "arbitrary")),
        name="conv_layer_carry" if carry else "conv_layer_state",
    )(x, state, w_in, b_in, dw_w, dw_b, cl_g, cl_b, w_out, ln_g, ln_b)
    return y, new_state


def _nsa_inproj_kernel(x_ref, w_ref, q_ref, kv_ref, sz_ref, g_ref, *, d_q, d_kv):
    xb = x_ref[...].astype(BF16)
    q_ref[...] = _dot(xb, w_ref[:, 0:d_q]).astype(BF16)
    kv_ref[...] = _dot(xb, w_ref[:, d_q:d_q + d_kv])
    z = _dot(xb, w_ref[:, d_q + d_kv:2 * d_q + d_kv])
    sz_ref[...] = z * _sigmoid(z)
    g_ref[...] = _sigmoid(_dot(xb, w_ref[:, 2 * d_q + d_kv:]))


def _nsa_inproj(x, w, *, d_q, d_kv, tm):
    N, D = x.shape
    d_g = w.shape[1] - 2 * d_q - d_kv
    row = lambda i: (i, 0)
    return pl.pallas_call(
        functools.partial(_nsa_inproj_kernel, d_q=d_q, d_kv=d_kv),
        grid=(N // tm,),
        in_specs=[pl.BlockSpec((tm, D), row), pl.BlockSpec(w.shape, lambda i: (0, 0))],
        out_specs=[pl.BlockSpec((tm, d_q), row), pl.BlockSpec((tm, d_kv), row),
                   pl.BlockSpec((tm, d_q), row), pl.BlockSpec((tm, d_g), row)],
        out_shape=[jax.ShapeDtypeStruct((N, d_q), BF16), jax.ShapeDtypeStruct((N, d_kv), F32),
                   jax.ShapeDtypeStruct((N, d_q), F32), jax.ShapeDtypeStruct((N, d_g), F32)],
        compiler_params=_cparams(("arbitrary",)),
        name="nsa_inproj",
    )(x, w)


def _nsa_inproj_prompt_kernel(x_ref, wn_ref, wt_ref, q_ref, sz_ref, g_ref, ksa_ref, kwa_ref, kvt_ref, va_ref,
                              *, d_q, kt_size):
    tm = x_ref.shape[1]
    t0 = pl.program_id(1) * tm
    xb = x_ref[0].astype(BF16)
    q_ref[0] = _dot(xb, wn_ref[:, 0:d_q]).astype(BF16)
    z = _dot(xb, wn_ref[:, d_q:2 * d_q])
    sz_ref[0] = z * _sigmoid(z)
    o = 2 * d_q
    g_ref[0] = _sigmoid(_dot(xb, wn_ref[:, o:o + LANES]))
    o += LANES
    kw = N_KV * K_AUG
    pos = t0 + lax.broadcasted_iota(jnp.int32, (tm, kw), 0)
    lane = lax.broadcasted_iota(jnp.int32, (tm, kw), 1) % K_AUG
    onehot = jnp.where(lane - HEAD_DIM == (pos % kt_size) // SEL_BLOCK, 1.0, 0.0)
    ksa_ref[0] = (_dot(xb, wn_ref[:, o:o + kw]) + onehot).astype(BF16)
    kwa_ref[0] = _dot(xb, wn_ref[:, o + kw:o + 2 * kw]).astype(BF16)
    hT = lax.dot_general(wt_ref[...], xb, (((1,), (1,)), ((), ())), preferred_element_type=F32)
    kvt_ref[0] = hT
    ones = jnp.concatenate([jnp.ones((1, tm), F32), jnp.zeros((V_AUG - HEAD_DIM - 1, tm), F32)], axis=0).astype(BF16)
    for i, st in enumerate((3, 5)):
        for g in range(N_KV):
            r0 = (st * N_KV + g) * HEAD_DIM
            va_ref[0, i, g, 0:HEAD_DIM, :] = hT[r0:r0 + HEAD_DIM].astype(BF16)
            va_ref[0, i, g, HEAD_DIM:V_AUG, :] = ones


def _nsa_inproj_prompt(x, w_nat, w_kvt, *, d_q, kt_size, tm):
    B, T, D = x.shape
    kw = N_KV * K_AUG
    tok = lambda n: pl.BlockSpec((1, tm, n), lambda b, t: (b, t, 0))
    const = lambda a: pl.BlockSpec(a.shape, lambda b, t: (0, 0))
    return pl.pallas_call(
        functools.partial(_nsa_inproj_prompt_kernel, d_q=d_q, kt_size=kt_size),
        grid=(B, T // tm),
        in_specs=[tok(D), const(w_nat), const(w_kvt)],
        out_specs=[tok(d_q), tok(d_q), tok(LANES), tok(kw), tok(kw),
                   pl.BlockSpec((1, w_kvt.shape[0], tm), lambda b, t: (b, 0, t)),
                   pl.BlockSpec((1, 2, N_KV, V_AUG, tm), lambda b, t: (b, 0, 0, 0, t))],
        out_shape=[jax.ShapeDtypeStruct((B, T, d_q), BF16), jax.ShapeDtypeStruct((B, T, d_q), F32),
                   jax.ShapeDtypeStruct((B, T, LANES), F32), jax.ShapeDtypeStruct((B, T, kw), BF16),
                   jax.ShapeDtypeStruct((B, T, kw), BF16), jax.ShapeDtypeStruct((B, w_kvt.shape[0], T), F32),
                   jax.ShapeDtypeStruct((B, 2, N_KV, V_AUG, T), BF16)],
        compiler_params=_cparams(("arbitrary", "arbitrary")),
        name="nsa_inproj_prompt",
    )(x, w_nat, w_kvt)


def _compress_finish(parts, base, w2):
    nchunk = parts.shape[0]
    second = pltpu.roll(parts[:, CMP_HID:], nchunk - 1, 0)
    row = lax.broadcasted_iota(jnp.int32, (nchunk, CMP_HID), 0)
    second = jnp.where(row < nchunk - 1, second, 0.0)
    h = base + parts[:, :CMP_HID] + second
    act = h * _sigmoid(h)
    return _dot(act.astype(BF16), w2).astype(BF16)


def _dot_f32_by_const(x, c, *, const_first):
    hi = x.astype(BF16)
    r1 = x - hi.astype(F32)
    mid = r1.astype(BF16)
    lo = (r1 - mid.astype(F32)).astype(BF16)
    out = None
    for part in (hi, mid, lo):
        d = _dot(c, part) if const_first else _dot(part, c)
        out = d if out is None else out + d
    return out


def _softmax_cols(s, valid):
    s = jnp.where(valid, s, NEG)
    m = jnp.max(s, axis=0, keepdims=True)
    e = jnp.where(valid, jnp.exp2(s - m), 0.0)
    return e / jnp.maximum(jnp.sum(e, axis=0, keepdims=True), 1e-30)


def _selection_scores(selimp, cur):
    j = lax.broadcasted_iota(jnp.int32, selimp.shape, 0)
    causal = j <= cur
    forced = (j == 0) | (j == cur) | (j == cur - 1)
    return jnp.where(forced, jnp.inf, jnp.where(causal, selimp, -jnp.inf)), causal


def _select_blocks(selimp, cur, n_select):
    ns = selimp.shape[0]
    j = lax.broadcasted_iota(jnp.int32, selimp.shape, 0)
    score, causal = _selection_scores(selimp, cur)
    taken = jnp.zeros(selimp.shape, jnp.int32)
    for _ in range(n_select):
        m = jnp.max(score, axis=0, keepdims=True)
        first = jnp.min(jnp.where(score == m, j, ns), axis=0, keepdims=True)
        hit = j == first
        taken = jnp.where(hit, 1, taken)
        score = jnp.where(hit, -jnp.inf, score)
    return (taken > 0) & causal


def _select_blocks_by_rank(selimp, cur, n_select, score_ref):
    ns = selimp.shape[0]
    score, causal = _selection_scores(selimp, cur)
    score_ref[...] = score
    sub = lax.broadcasted_iota(jnp.int32, (8, selimp.shape[1]), 0)
    ranks = []
    for v in range(ns // 8):
        blk = score[8 * v:8 * v + 8]
        rank = jnp.zeros(blk.shape, F32)
        for jp in range(ns):
            row = score_ref[jp:jp + 1, :]
            if jp < 8 * v:
                ahead = row >= blk
            elif jp >= 8 * v + 8:
                ahead = row > blk
            else:
                ahead = (row > blk) | ((row == blk) & (sub > jp - 8 * v))
            rank = rank + jnp.where(ahead, 1.0, 0.0)
        ranks.append(rank)
    return (jnp.concatenate(ranks, axis=0) < n_select) & causal


def _cmp_to_sel_matrix(n_sel, n_cmp):
    r = SEL_BLOCK // CMP_STRIDE
    lo = -(CMP_BLOCK // CMP_STRIDE - 1)
    m = np.zeros((n_sel, n_cmp), np.float32)
    for o in range(lo, r):
        start = o * CMP_STRIDE
        ov = max(0, min(SEL_BLOCK, start + CMP_BLOCK) - max(0, start))
        for jb in range(n_sel):
            n = r * jb + o
            if ov > 0 and 0 <= n < n_cmp:
                m[jb, n] += ov / CMP_BLOCK
    return m


V_AUG = 80
K_AUG = 128
BIAS_ROWS = 16


def _prompt_attn_kernel(q_ref, g_ref, kc_ref, vct_ref, ksa_ref, vsa_ref, kwa_ref, vwa_ref, mt_ref, o_ref,
                        bias_ref, score_ref, *, tq, kt_size, gps):
    qi = pl.program_id(1)
    qs = qi * tq
    hp = GROUP // 2
    RH = 2 * tq
    n_cmp = kc_ref.shape[2]
    n_sel = mt_ref.shape[0]
    blk_per_tile = kt_size // SEL_BLOCK
    chains = [(gi, h) for gi in range(gps) for h in range(hp)]

    t1 = qs + lax.broadcasted_iota(jnp.int32, (1, tq), 1)
    t_lane = jnp.concatenate([t1, t1], axis=1)
    pair = lambda gi, h: slice((gi * hp + h) * 2 * HEAD_DIM, (gi * hp + h + 1) * 2 * HEAD_DIM)
    qT = {}
    for gi, h in chains:
        t2 = q_ref[0, :, pair(gi, h)].astype(F32).T.astype(BF16)
        qT[gi, h] = jnp.concatenate([t2[0:HEAD_DIM], t2[HEAD_DIM:2 * HEAD_DIM]], axis=1)
    gT = g_ref[0].T
    k_lanes = lambda gi: slice(gi * K_AUG, (gi + 1) * K_AUG)

    wk = WINDOW + tq
    start = pl.multiple_of(jnp.maximum(qs - WINDOW, 0), 128)
    s_cmp = {c: _dot(kc_ref[0, c[0]], qT[c]) for c in chains}
    k_pad = jnp.zeros((K_AUG - HEAD_DIM, RH), BF16)
    s_win = {c: _dot(kwa_ref[0, pl.ds(start, wk), k_lanes(c[0])], jnp.concatenate([qT[c], k_pad], axis=0))
             for c in chains}

    n_idx = lax.broadcasted_iota(jnp.int32, (n_cmp, RH), 0)
    cmp_valid = n_idx * CMP_STRIDE + (CMP_BLOCK - 1) <= t_lane
    o_cmp = {}
    imps = [None] * gps
    for gi, h in chains:
        p_cmp = _softmax_cols(s_cmp[gi, h], cmp_valid)
        o_cmp[gi, h] = _dot(vct_ref[0, gi], p_cmp.astype(BF16))
        part = p_cmp[:, 0:tq] + p_cmp[:, tq:2 * tq]
        imps[gi] = part if imps[gi] is None else imps[gi] + part

    dpos = t_lane - (start + lax.broadcasted_iota(jnp.int32, (wk, RH), 0))
    win_valid = (dpos >= 0) & (dpos < WINDOW)
    o_win = {}
    for gi, h in chains:
        sw = jnp.where(win_valid, s_win[gi, h], NEG)
        p_win = jnp.exp2(sw - jnp.max(sw, axis=0, keepdims=True)).astype(BF16)
        acc_win = _dot(vwa_ref[0, 0, gi, :, pl.ds(start, wk)], p_win)
        o_win[gi, h] = acc_win[0:HEAD_DIM] / jnp.maximum(acc_win[HEAD_DIM:HEAD_DIM + 1], 1e-30)

    pad = jnp.zeros((BIAS_ROWS - blk_per_tile, tq), F32)
    for gi in range(gps):
        selimp = _dot_f32_by_const(imps[gi], mt_ref[...], const_first=True)
        sel = _select_blocks_by_rank(selimp, t1 // SEL_BLOCK, N_SELECT, score_ref.at[gi])
        bias = jnp.where(sel, 0.0, NEG)
        rows = []
        for kt in range(n_sel // blk_per_tile):
            rows += [bias[kt * blk_per_tile:(kt + 1) * blk_per_tile], pad]
        bias_rows = jnp.concatenate(rows, axis=0)
        bias_ref[gi] = jnp.concatenate([bias_rows, bias_rows], axis=1).astype(BF16)

    q_pad = jnp.zeros((K_AUG - HEAD_DIM - BIAS_ROWS, RH), BF16)

    def sel_units(kt, carries, diagonal):
        k0 = pl.multiple_of(kt * kt_size, kt_size)
        scores = []
        for gi, h in chains:
            b_t = bias_ref[gi, pl.ds(pl.multiple_of(kt * BIAS_ROWS, BIAS_ROWS), BIAS_ROWS), :]
            qa = jnp.concatenate([qT[gi, h], b_t, q_pad], axis=0)
            scores.append(_dot(ksa_ref[0, pl.ds(k0, kt_size), k_lanes(gi)], qa))
        out = []
        for (gi, h), s, (m, acc) in zip(chains, scores, carries):
            if diagonal:
                kidx = k0 + lax.broadcasted_iota(jnp.int32, (kt_size, RH), 0)
                s = jnp.where(kidx <= t_lane, s, NEG)
            m_new = jnp.maximum(m, jnp.max(s, axis=0, keepdims=True))
            p = jnp.exp2(s - m_new).astype(BF16)
            acc = jnp.exp2(m - m_new) * acc + _dot(vsa_ref[0, 0, gi, :, pl.ds(k0, kt_size)], p)
            out.append((m_new, acc))
        return tuple(out)

    n_full = qs // kt_size
    init = tuple((jnp.full((1, RH), NEG, F32), jnp.zeros((V_AUG, RH), F32)) for _ in chains)
    carries = lax.fori_loop(0, n_full, lambda kt, c: sel_units(kt, c, False), init)
    carries = sel_units(n_full, carries, True)

    def gate(gi, h, branch):
        r0 = branch * N_KV * GROUP + gi * GROUP + 2 * h
        return jnp.concatenate([gT[r0:r0 + 1], gT[r0 + 1:r0 + 2]], axis=1)

    for i, (gi, h) in enumerate(chains):
        acc_sel = carries[i][1]
        o_sel = acc_sel[0:HEAD_DIM] / jnp.maximum(acc_sel[HEAD_DIM:HEAD_DIM + 1], 1e-30)
        o = gate(gi, h, 0) * o_cmp[gi, h] + gate(gi, h, 1) * o_sel + gate(gi, h, 2) * o_win[gi, h]
        o_ref[0, :, pair(gi, h)] = jnp.concatenate([o[:, 0:tq], o[:, tq:2 * tq]], axis=0).T


def _prompt_attn(q, gates, kc, vcT, ksa, kwa, va, mT, *, tq, kt_size):
    B, T, att_w = q.shape
    G, NC, Dh = kc.shape[1:]
    n_sel = mT.shape[0]
    assert kt_size % SEL_BLOCK == 0 and kt_size // SEL_BLOCK <= BIAS_ROWS and T % kt_size == 0 and kt_size % tq == 0
    assert tq == LANES and G == N_KV
    whole = lambda a: pl.BlockSpec((1,) + a.shape[1:], lambda b, i: (b,) + (0,) * (a.ndim - 1))
    v_spec = lambda s: pl.BlockSpec((1, 1, G, V_AUG, T), lambda b, i: (b, s, 0, 0, 0))
    return pl.pallas_call(
        functools.partial(_prompt_attn_kernel, tq=tq, kt_size=kt_size, gps=G),
        grid=(B, T // tq),
        in_specs=[
            pl.BlockSpec((1, tq, att_w), lambda b, i: (b, i, 0)),
            pl.BlockSpec((1, tq, gates.shape[2]), lambda b, i: (b, i, 0)),
            whole(kc), whole(vcT), whole(ksa), v_spec(0), whole(kwa), v_spec(1),
            pl.BlockSpec(mT.shape, lambda b, i: (0, 0)),
        ],
        out_specs=pl.BlockSpec((1, tq, att_w), lambda b, i: (b, i, 0)),
        out_shape=jax.ShapeDtypeStruct((B, T, att_w), F32),
        scratch_shapes=[pltpu.VMEM((G, T // kt_size * BIAS_ROWS, 2 * tq), BF16),
                        pltpu.VMEM((G, n_sel, tq), F32)],
        compiler_params=_cparams(("arbitrary", "arbitrary")),
        name="prompt_attn",
    )(q, gates, kc, vcT, ksa, va, kwa, va, mT)


def _dot_tn(v, p):
    return lax.dot_general(v, p, (((0,), (0,)), ((), ())), preferred_element_type=F32)


PAGES_PER_STEP = 16
PANEL_PITCH = CMP_STRIDE + 1


def _page_specs(block_streams, stream_block, page):
    def spec(j):
        return pl.BlockSpec((1, block_streams, N_KV, HEAD_DIM, page),
                            lambda b, kt, pt: (pt[b, kt * PAGES_PER_STEP + j], stream_block, 0, 0, 0))
    return spec


def _sample_compress_kernel(pt_ref, *refs, page):
    del pt_ref
    pages = refs[:PAGES_PER_STEP]
    pe_ref, w1_ref, wp_ref, b1_ref, w2_ref, o_ref, panel, xs = refs[PAGES_PER_STEP:]
    kt = pl.program_id(1)
    cps = (page // CMP_STRIDE) * PAGES_PER_STEP
    row0 = pl.multiple_of(kt * cps, cps)
    for st in range(2):
        for gp in range(N_KV // 2):
            for j, p_ref in enumerate(pages):
                tile = p_ref[0, st, 2 * gp:2 * gp + 2].reshape(2 * HEAD_DIM, page).T
                for c in range(page // CMP_STRIDE):
                    r0 = (j * (page // CMP_STRIDE) + c) * PANEL_PITCH
                    panel[pl.ds(r0, CMP_STRIDE, stride=1), :] = tile[c * CMP_STRIDE:(c + 1) * CMP_STRIDE]
            for s in range(CMP_STRIDE):
                xs[st, gp, s, pl.ds(row0, cps), :] = panel[pl.ds(s, cps, stride=PANEL_PITCH), :].astype(BF16)

    @pl.when(kt == pl.num_programs(1) - 1)
    def _():
        for st in range(2):
            base = _dot(pe_ref[st], w1_ref[st])[0:1, :] + b1_ref[st]
            for gp in range(N_KV // 2):
                lhs = jnp.concatenate([xs[st, gp, s] for s in range(CMP_STRIDE)], axis=1)
                acc = _dot(lhs, wp_ref[st])
                for g2 in range(2):
                    parts = acc[:, g2 * 2 * CMP_HID:(g2 + 1) * 2 * CMP_HID]
                    o_ref[st, 0, 2 * gp + g2] = _compress_finish(parts, base, w2_ref[st])


def _sample_compress(cache_t, page_table, pe, w1, wp, b1, w2, *, name="sample_compress"):
    page = 2 * HEAD_DIM
    if page_table is None:
        DB, n_pages = cache_t.shape[0], cache_t.shape[-1] // page
        page_table = jnp.zeros((1, 1), jnp.int32)

        def spec(j):
            return pl.BlockSpec((1, 2, N_KV, HEAD_DIM, page), lambda b, kt, pt: (b, 0, 0, 0, kt * PAGES_PER_STEP + j))
    else:
        DB, n_pages = page_table.shape
        assert cache_t.shape[-1] == page
        spec = _page_specs(2, 0, page)
    nchunk = n_pages * page // CMP_STRIDE
    assert n_pages % PAGES_PER_STEP == 0 and page == LANES
    const = lambda a: pl.BlockSpec(a.shape, lambda b, kt, pt: (0,) * a.ndim)
    return pl.pallas_call(
        functools.partial(_sample_compress_kernel, page=page),
        grid_spec=pltpu.PrefetchScalarGridSpec(
            num_scalar_prefetch=1,
            grid=(DB, n_pages // PAGES_PER_STEP),
            in_specs=[spec(j) for j in range(PAGES_PER_STEP)] + [const(pe), const(w1), const(wp), const(b1), const(w2)],
            out_specs=pl.BlockSpec((2, 1, N_KV, nchunk, HEAD_DIM), lambda b, kt, pt: (0, b, 0, 0, 0)),
            scratch_shapes=[pltpu.VMEM((PAGES_PER_STEP * page // CMP_STRIDE * PANEL_PITCH, 2 * HEAD_DIM), F32),
                            pltpu.VMEM((2, N_KV // 2, CMP_STRIDE, nchunk, 2 * HEAD_DIM), BF16)],
        ),
        out_shape=jax.ShapeDtypeStruct((2, DB, N_KV, nchunk, HEAD_DIM), BF16),
        compiler_params=_cparams(("arbitrary", "arbitrary")),
        name=name,
    )(page_table, *([cache_t] * PAGES_PER_STEP), pe, w1, wp, b1, w2)


def _sample_attn_kernel(pt_ref, *refs, t_new, past, page):
    del pt_ref
    pages = refs[:PAGES_PER_STEP]
    (q_ref, g_ref, kc_ref, vc_ref, kn_ref, vn_ref, kw_ref, vw_ref, kwn_ref, vwn_ref, mt_ref, rr_ref,
     o_ref, bias_ref, m_ref, l_ref, acc_ref, oc_ref) = refs[PAGES_PER_STEP:]
    kt = pl.program_id(1)
    gd = N_KV * HEAD_DIM
    R = N_KV * GROUP * t_new
    qbd = q_ref[0]
    lane = lax.broadcasted_iota(jnp.int32, (1, R), 1)
    i_lane = lane % t_new
    t_lane = past + i_lane

    @pl.when(kt == 0)
    def _():
        n_cmp = kc_ref.shape[2]
        sc = None
        for g in range(N_KV):
            d = _dot(kc_ref[0, g], qbd[g * HEAD_DIM:(g + 1) * HEAD_DIM, :])
            sc = d if sc is None else sc + d
        n_idx = lax.broadcasted_iota(jnp.int32, (n_cmp, R), 0)
        p_cmp = _softmax_cols(sc, n_idx * CMP_STRIDE + (CMP_BLOCK - 1) <= t_lane)
        pb = p_cmp.astype(BF16)
        o_cmp = jnp.concatenate([_dot_tn(vc_ref[0, g], pb) for g in range(N_KV)], axis=0)

        imp = _dot_f32_by_const(p_cmp, rr_ref[...], const_first=False)
        selimp = _dot_f32_by_const(imp, mt_ref[...], const_first=True)
        sel = _select_blocks(selimp, t_lane // SEL_BLOCK, N_SELECT)
        bias_ref[...] = jnp.where(sel, 0.0, NEG)

        wbuf = kw_ref.shape[-1]
        sw = jnp.concatenate([_dot_tn(kw_ref[0].astype(BF16), qbd), _dot(kwn_ref[0], qbd)], axis=0)
        jw = lax.broadcasted_iota(jnp.int32, (wbuf + t_new, R), 0)
        dpos = wbuf + i_lane - jw
        p_win = _softmax_cols(sw, (dpos >= 0) & (dpos < WINDOW) & (past - wbuf + jw >= 0)).astype(BF16)
        o_win = _dot(vw_ref[0].astype(BF16), p_win[0:wbuf]) + _dot_tn(vwn_ref[0], p_win[wbuf:wbuf + t_new])
        oc_ref[...] = g_ref[0, 0:1, :] * o_cmp + g_ref[0, 2:3, :] * o_win

        new_blk = past // SEL_BLOCK
        s_new = _dot(kn_ref[0], qbd) + bias_ref[new_blk:new_blk + 1, :]
        ip = lax.broadcasted_iota(jnp.int32, (t_new, R), 0)
        s_new = jnp.where(ip <= i_lane, s_new, NEG)
        m0 = jnp.max(s_new, axis=0, keepdims=True)
        p0 = jnp.exp2(s_new - m0)
        m_ref[0] = m0
        l_ref[0] = jnp.sum(p0, axis=0, keepdims=True)
        acc_ref[0] = _dot_tn(vn_ref[0], p0.astype(BF16))
        m_ref[1] = jnp.full((1, R), NEG, F32)
        l_ref[1] = jnp.zeros((1, R), F32)
        acc_ref[1] = jnp.zeros((gd, R), F32)

    blk_per_page = page // SEL_BLOCK
    nb = PAGES_PER_STEP * blk_per_page
    b_t = bias_ref[pl.ds(pl.multiple_of(kt * nb, nb), nb), :]
    ppc = PAGES_PER_STEP // 2
    kpc = ppc * page
    scores = []
    for c in range(2):
        kT = jnp.concatenate([p_ref[0, 0].reshape(gd, page).astype(BF16) for p_ref in pages[c * ppc:(c + 1) * ppc]], axis=1)
        scores.append(jnp.concatenate([_dot_tn(kT[:, 0:kpc // 2], qbd), _dot_tn(kT[:, kpc // 2:], qbd)], axis=0))
    state = []
    for c in range(2):
        vT = jnp.concatenate([p_ref[0, 1].reshape(gd, page).astype(BF16) for p_ref in pages[c * ppc:(c + 1) * ppc]], axis=1)
        nbc = nb // 2
        s = jnp.concatenate([scores[c][jb * SEL_BLOCK:(jb + 1) * SEL_BLOCK] + b_t[c * nbc + jb:c * nbc + jb + 1, :]
                             for jb in range(nbc)], axis=0)
        m_old = m_ref[c]
        m_new = jnp.maximum(m_old, jnp.max(s, axis=0, keepdims=True))
        alpha = jnp.exp2(m_old - m_new)
        p = jnp.exp2(s - m_new)
        l = alpha * l_ref[c] + jnp.sum(p, axis=0, keepdims=True)
        pb = p.astype(BF16)
        acc = alpha * acc_ref[c] + jnp.concatenate([_dot(vT[0:gd // 2], pb), _dot(vT[gd // 2:], pb)], axis=0)
        m_ref[c] = m_new
        l_ref[c] = l
        acc_ref[c] = acc
        state.append((m_new, l, acc))

    @pl.when(kt == pl.num_programs(1) - 1)
    def _():
        (m_a, l_a, acc_a), (m_b, l_b, acc_b) = state
        m = jnp.maximum(m_a, m_b)
        w_a, w_b = jnp.exp2(m_a - m), jnp.exp2(m_b - m)
        l = w_a * l_a + w_b * l_b
        acc = w_a * acc_a + w_b * acc_b
        o = oc_ref[...] + g_ref[0, 1:2, :] * (acc / jnp.maximum(l, 1e-30))
        grp = lane // (GROUP * t_new)
        out = jnp.zeros((HEAD_DIM, R), F32)
        for g in range(N_KV):
            out = out + jnp.where(grp == g, o[g * HEAD_DIM:(g + 1) * HEAD_DIM, :], 0.0)
        o_ref[0] = out


def _sample_attn(cache_t, page_table, qbd, gT, kcv, kn, vn, kw, vw, kwn, vwn, mT, rr, *, t_new):
    page = cache_t.shape[-1]
    DB, n_pages = page_table.shape
    past = n_pages * page
    R = qbd.shape[-1]
    per_b = lambda a: pl.BlockSpec((1,) + a.shape[1:], lambda b, kt, pt: (b,) + (0,) * (a.ndim - 1))
    const = lambda a: pl.BlockSpec(a.shape, lambda b, kt, pt: (0,) * a.ndim)
    cmp_spec = lambda st: pl.BlockSpec((None, 1) + kcv.shape[2:], lambda b, kt, pt: (st, b, 0, 0, 0))
    spec = _page_specs(2, 1, page)
    return pl.pallas_call(
        functools.partial(_sample_attn_kernel, t_new=t_new, past=past, page=page),
        grid_spec=pltpu.PrefetchScalarGridSpec(
            num_scalar_prefetch=1,
            grid=(DB, n_pages // PAGES_PER_STEP),
            in_specs=[spec(j) for j in range(PAGES_PER_STEP)]
            + [per_b(qbd), per_b(gT), cmp_spec(0), cmp_spec(1), per_b(kn), per_b(vn), per_b(kw), per_b(vw),
               per_b(kwn), per_b(vwn), const(mT), const(rr)],
            out_specs=pl.BlockSpec((1, HEAD_DIM, R), lambda b, kt, pt: (b, 0, 0)),
            scratch_shapes=[pltpu.VMEM((mT.shape[0], R), F32), pltpu.VMEM((2, 1, R), F32), pltpu.VMEM((2, 1, R), F32),
                            pltpu.VMEM((2, N_KV * HEAD_DIM, R), F32), pltpu.VMEM((N_KV * HEAD_DIM, R), F32)],
        ),
        out_shape=jax.ShapeDtypeStruct((DB, HEAD_DIM, R), F32),
        compiler_params=_cparams(("arbitrary", "arbitrary")),
        name="sample_attn",
    )(page_table, *([cache_t] * PAGES_PER_STEP), qbd, gT, kcv, kcv, kn, vn, kw, vw, kwn, vwn, mT, rr)


def _nsa_outproj_kernel(o_ref, sz_ref, x_ref, w_ref, g_ref, b_ref, y_ref):
    gated = (o_ref[...] * sz_ref[...]).astype(BF16)
    y = _dot(gated, w_ref[...])
    y_ref[...] = _layer_norm(ALPHA * x_ref[...] + y, g_ref[...], b_ref[...])


def _nsa_outproj(o, sz, x, w, g, b, *, tm):
    N, D = x.shape
    row = lambda i: (i, 0)
    const = lambda i: (0, 0)
    return pl.pallas_call(
        _nsa_outproj_kernel,
        grid=(N // tm,),
        in_specs=[pl.BlockSpec((tm, o.shape[1]), row), pl.BlockSpec((tm, sz.shape[1]), row),
                  pl.BlockSpec((tm, D), row), pl.BlockSpec(w.shape, const),
                  pl.BlockSpec(g.shape, const), pl.BlockSpec(b.shape, const)],
        out_specs=pl.BlockSpec((tm, D), row),
        out_shape=jax.ShapeDtypeStruct((N, D), F32),
        compiler_params=_cparams(("arbitrary",)),
        name="nsa_outproj",
    )(o, sz, x, w, g, b)


def _prep_nsa_weights(w_in, pe, w1, b1, w2):
    D = w_in.shape[0]
    att_w = N_KV * GROUP * HEAD_DIM
    kv_w = 6 * N_KV * HEAD_DIM
    o1, o2 = att_w, att_w + kv_w
    o3 = o2 + 3 * N_KV * GROUP
    w_q = w_in[:, :o1] * (HEAD_DIM ** -0.5 * np.log2(np.e))
    w_g = w_in[:, o2:o3].reshape(D, N_KV, GROUP, 3).transpose(0, 3, 1, 2).reshape(D, 3 * N_KV * GROUP)
    w_g = jnp.pad(w_g, ((0, 0), (0, 128 - w_g.shape[1])))
    w_all = jnp.concatenate([w_q, w_in[:, o1:o2], w_in[:, o3:], w_g], axis=1).astype(BF16)
    kv_cols = w_in[:, o1:o2].reshape(D, 6, N_KV, HEAD_DIM)
    padded = lambda st: jnp.pad(kv_cols[:, st], ((0, 0), (0, 0), (0, K_AUG - HEAD_DIM))).reshape(D, N_KV * K_AUG)
    w_nat = jnp.concatenate([w_q, w_in[:, o3:], w_g, padded(2), padded(4)], axis=1).astype(BF16)
    w_kvt = w_in[:, o1:o2].T.astype(BF16)
    r = CMP_BLOCK // CMP_STRIDE
    pe_rows = jnp.pad(pe.reshape(2, 1, CMP_BLOCK * HEAD_DIM), ((0, 0), (0, 7), (0, 0))).astype(BF16)
    w1_flat = w1.reshape(2, CMP_BLOCK * HEAD_DIM, CMP_HID).astype(BF16)
    w1p = w1.reshape(2, r, CMP_STRIDE, HEAD_DIM, CMP_HID).transpose(0, 2, 3, 1, 4)
    w1p = w1p.reshape(2, CMP_STRIDE, HEAD_DIM, r * CMP_HID)
    eye = jnp.eye(2, dtype=w1.dtype)
    w1_bd = w1p[:, :, None, :, None, :] * eye[None, None, :, None, :, None]
    w1_bd = w1_bd.reshape(2, CMP_STRIDE * 2 * HEAD_DIM, 2 * r * CMP_HID).astype(BF16)
    return (w_all, w_nat, w_kvt), pe_rows, w1_flat, w1_bd, b1.reshape(2, 1, CMP_HID), w2.astype(BF16)


def _nsa_prompt(x1, nsa_w, w_out, g, b):
    B, T, D = x1.shape
    (_, w_nat, w_kvt), pe_rows, w1_flat, w1_bd, b1, w2 = nsa_w
    att_w = N_KV * GROUP * HEAD_DIM
    kt_size = 512
    q, sz, gates, ksa, kwa, kvt, va = _nsa_inproj_prompt(x1, w_nat, w_kvt, d_q=att_w, kt_size=kt_size, tm=512)
    kvt6 = kvt.reshape(B, 6, N_KV, HEAD_DIM, T)
    rows = kvt6[:, :4].transpose(0, 4, 1, 2, 3)
    win = kvt6[:, 4:, :, :, T - min(WINDOW, T):].transpose(0, 4, 1, 2, 3)
    nchunk = T // CMP_STRIDE
    kcv = _sample_compress(kvt6, None, pe_rows, w1_flat, w1_bd, b1, w2, name="prompt_compress")
    mT = jnp.asarray(_cmp_to_sel_matrix(T // SEL_BLOCK, nchunk), BF16)
    o = _prompt_attn(q, gates, kcv[0], kcv[1].transpose(0, 1, 3, 2), ksa, kwa, va, mT, tq=128, kt_size=kt_size)
    y = _nsa_outproj(o.reshape(B * T, att_w), sz.reshape(B * T, att_w), x1.reshape(B * T, D), w_out, g, b, tm=512)
    return y.reshape(B, T, D), rows, win


def _nsa_sample(x1, cache_kv_l, cache_win_l, page_table, nsa_w, w_out, g, b):
    DB, T, D = x1.shape
    (w_all, _, _), pe_rows, w1_flat, w1_bd, b1, w2 = nsa_w
    att_w = N_KV * GROUP * HEAD_DIM
    kv_w = 6 * N_KV * HEAD_DIM
    page = cache_kv_l.shape[1]
    past = page_table.shape[1] * page
    assert T <= CMP_STRIDE and past % SEL_BLOCK == 0 and past % CMP_STRIDE == 0
    x_flat = x1.reshape(DB * T, D)
    q, kv, sz, gates = _nsa_inproj(x_flat, w_all, d_q=att_w, d_kv=kv_w, tm=DB * T)
    kv6 = kv.reshape(DB, T, 6, N_KV, HEAD_DIM)
    rows = kv6[:, :, :4]
    wseq = jnp.concatenate([cache_win_l, kv6[:, :, 4:]], axis=1)
    gd = N_KV * HEAD_DIM
    R = N_KV * GROUP * T
    qT = q.reshape(DB, T, N_KV, GROUP, HEAD_DIM).transpose(0, 2, 4, 3, 1)
    eye = jnp.eye(N_KV, dtype=BF16)
    qbd = (qT[:, :, :, None] * eye[None, :, None, :, None, None]).reshape(DB, gd, R)
    gT = gates[:, :3 * N_KV * GROUP].reshape(DB, T, 3, N_KV * GROUP).transpose(0, 2, 3, 1).reshape(DB, 3, R)
    gT = jnp.pad(gT, ((0, 0), (0, 5), (0, 0)))
    new_rows = lambda st: kv6[:, :, st].reshape(DB, T, gd).astype(BF16)
    cache_t = cache_kv_l.transpose(0, 2, 3, 4, 1)
    win_t = cache_win_l.transpose(0, 2, 3, 4, 1)
    wbuf = win_t.shape[-1]
    nchunk = past // CMP_STRIDE
    kcv = _sample_compress(cache_t, page_table, pe_rows, w1_flat, w1_bd, b1, w2)
    n_sel = past // SEL_BLOCK + 1
    n_sel_pad = -(-n_sel // (PAGES_PER_STEP * page // SEL_BLOCK)) * (PAGES_PER_STEP * page // SEL_BLOCK)
    mT = np.zeros((n_sel_pad, nchunk), np.float32)
    mT[:n_sel] = _cmp_to_sel_matrix(n_sel, nchunk)
    lane = np.arange(R)
    rr = ((lane[:, None] % T == lane[None, :] % T)
          & (lane[:, None] // (GROUP * T) == lane[None, :] // (GROUP * T))).astype(np.float32)
    oT = _sample_attn(cache_t, page_table, qbd, gT, kcv, new_rows(2), new_rows(3),
                      win_t[:, 0].reshape(DB, gd, wbuf), win_t[:, 1].reshape(DB, gd, wbuf), new_rows(4), new_rows(5),
                      jnp.asarray(mT, BF16), jnp.asarray(rr, BF16), t_new=T)
    o = oT.reshape(DB, HEAD_DIM, N_KV, GROUP, T).transpose(0, 4, 2, 3, 1).reshape(DB * T, att_w)
    y = _nsa_outproj(o, sz, x_flat, w_out, g, b, tm=DB * T)
    return y.reshape(DB, T, D), rows, wseq[:, T:]


def kernel(x_prompt, x_sample, state_conv, cache_kv, cache_win, page_table, conv_w_in, conv_b_in, conv_dw_w,
           conv_dw_b, conv_ln_g, conv_ln_b, conv_w_out, nsa_w_in, cmp_pe, cmp_w1, cmp_b1, cmp_w2, nsa_w_out,
           ln_g, ln_b):
    B, T, D = x_prompt.shape
    DB, TS, _ = x_sample.shape
    d_in = conv_w_out.shape[1]
    row = lambda v: v.reshape(1, -1)

    cw = (conv_w_in[0].astype(BF16), row(conv_b_in[0]), conv_dw_w[0], row(conv_dw_b[0]), row(conv_ln_g[0]),
          row(conv_ln_b[0]), conv_w_out[0].astype(BF16), row(ln_g[0]), row(ln_b[0]))
    pad = HALO - (CONV_W - 1)
    xp, sp = _conv_layer(x_prompt, jnp.zeros((B, HALO, d_in), F32), *cw, ts=256, carry=True)
    st = jnp.pad(state_conv[0], ((0, 0), (pad, 0), (0, 0)))
    xs, ss = _conv_layer(x_sample, st, *cw, ts=TS, carry=False)

    nsa_w = _prep_nsa_weights(nsa_w_in[0], cmp_pe[0], cmp_w1[0], cmp_b1[0], cmp_w2[0])
    w_out = nsa_w_out[0].astype(BF16)
    yp, rp, wp = _nsa_prompt(xp, nsa_w, w_out, row(ln_g[1]), row(ln_b[1]))
    ys, rs, ws = _nsa_sample(xs, cache_kv[0], cache_win[0], page_table, nsa_w, w_out, row(ln_g[1]), row(ln_b[1]))
    return (yp, ys, sp[None, :, pad:], ss[None, :, pad:], rp[None], rs[None], wp[None], ws[None])
```

```python
import functools

import jax
import jax.numpy as jnp
import numpy as np
from jax import lax
from jax.experimental import pallas as pl
from jax.experimental.pallas import tpu as pltpu

F32 = jnp.float32
BF16 = jnp.bfloat16

CONV_W = 31
N_KV = 4
GROUP = 4
HEAD_DIM = 64
CMP_BLOCK = 32
CMP_STRIDE = 16
CMP_HID = 128
SEL_BLOCK = 64
N_SELECT = 16
WINDOW = 512
DEPTH = 2
ALPHA = (2 * DEPTH) ** 0.25
LN_EPS = 1e-5
NEG = -1e30

VMEM_LIMIT = 56 * 1024 * 1024
LANES = 128
HALO = 32


def _cparams(sem):
    return pltpu.CompilerParams(dimension_semantics=sem, vmem_limit_bytes=VMEM_LIMIT)


def _sigmoid(x):
    return 1.0 / (1.0 + jnp.exp(-x))


def _layer_norm(x, g, b):
    mu = jnp.mean(x, axis=-1, keepdims=True)
    xc = x - mu
    var = jnp.mean(xc * xc, axis=-1, keepdims=True)
    return xc * lax.rsqrt(var + LN_EPS) * g + b


def _dot(a, b):
    return jnp.dot(a, b, preferred_element_type=F32)


def _conv_layer_kernel(x_ref, st_ref, win_ref, bin_ref, dww_ref, dwb_ref, clg_ref, clb_ref, wout_ref,
                       lng_ref, lnb_ref, y_ref, ns_ref, ubuf, zbuf, cbuf, *, ns, ts, d_in, carry):
    t = pl.program_id(1)
    rows = ns * ts
    cb = LANES
    n_cb = d_in // cb
    lanes = lambda ci: slice(ci * cb, (ci + 1) * cb)

    if carry:
        @pl.when(t == 0)
        def _():
            ubuf[:, :, 0:HALO, :] = jnp.zeros((ns, n_cb, HALO, cb), F32)
    else:
        for ci in range(n_cb):
            ubuf[:, ci, 0:HALO, :] = st_ref[:, :, lanes(ci)]

    x = x_ref[...].reshape(rows, x_ref.shape[-1])
    xb = x.astype(BF16)
    a = _dot(xb, win_ref[:, 0:d_in]) + bin_ref[:, 0:d_in]
    gl = _dot(xb, win_ref[:, d_in:2 * d_in]) + bin_ref[:, d_in:2 * d_in]
    u = (a * _sigmoid(gl)).reshape(ns, ts, d_in)
    for ci in range(n_cb):
        ubuf[:, ci, HALO:HALO + ts, :] = u[:, :, lanes(ci)]
    z = _dot(xb, win_ref[:, 2 * d_in:3 * d_in]) + bin_ref[:, 2 * d_in:3 * d_in]
    zbuf[...] = z * _sigmoid(z)

    rb = min(ts, 32)
    n_rb = ts // rb
    off0 = HALO - (CONV_W - 1)

    cpi = 4

    def chunk(i, c):
        s = i // (n_rb * n_cb // cpi)
        rem = i % (n_rb * n_cb // cpi)
        r0 = pl.multiple_of((rem // (n_cb // cpi)) * rb, rb)
        for e in range(cpi):
            ci = (rem % (n_cb // cpi)) * cpi + e
            c0 = pl.multiple_of(ci * cb, cb)
            acc = jnp.zeros((rb, cb), F32) + dwb_ref[:, pl.ds(c0, cb)]
            for k in range(CONV_W):
                acc = acc + dww_ref[k:k + 1, pl.ds(c0, cb)] * ubuf[s, ci, pl.ds(r0 + off0 + k, rb, stride=1), :]
            cbuf[s, pl.ds(r0, rb), pl.ds(c0, cb)] = acc
        return c

    lax.fori_loop(0, ns * n_rb * n_cb // cpi, chunk, 0)

    for ci in range(n_cb):
        ns_ref[:, :, lanes(ci)] = ubuf[:, ci, ts:ts + HALO, :]
    if carry:
        ubuf[:, :, 0:HALO, :] = ubuf[:, :, ts:ts + HALO, :]

    yc = _layer_norm(cbuf[...].reshape(rows, d_in), clg_ref[...], clb_ref[...])
    gated = (yc * _sigmoid(yc)) * zbuf[...]
    out = _dot(gated.astype(BF16), wout_ref[...])
    y = _layer_norm(ALPHA * x + out, lng_ref[...], lnb_ref[...])
    y_ref[...] = y.reshape(y_ref.shape)


def _conv_layer(x, state, w_in, b_in, dw_w, dw_b, cl_g, cl_b, w_out, ln_g, ln_b, *, ts, carry):
    S, T, D = x.shape
    d_in = w_out.shape[0]
    ns = 1 if carry else S
    grid = (S // ns, T // ts)
    const = lambda s, t: (0, 0)
    kernel = functools.partial(_conv_layer_kernel, ns=ns, ts=ts, d_in=d_in, carry=carry)
    y, new_state = pl.pallas_call(
        kernel,
        grid=grid,
        in_specs=[
            pl.BlockSpec((ns, ts, D), lambda s, t: (s, t, 0)),
            pl.BlockSpec((ns, HALO, d_in), lambda s, t: (s, 0, 0)),
            pl.BlockSpec(w_in.shape, const),
            pl.BlockSpec(b_in.shape, const),
            pl.BlockSpec(dw_w.shape, const),
            pl.BlockSpec(dw_b.shape, const),
            pl.BlockSpec(cl_g.shape, const),
            pl.BlockSpec(cl_b.shape, const),
            pl.BlockSpec(w_out.shape, const),
            pl.BlockSpec(ln_g.shape, const),
            pl.BlockSpec(ln_b.shape, const),
        ],
        out_specs=[
            pl.BlockSpec((ns, ts, D), lambda s, t: (s, t, 0)),
            pl.BlockSpec((ns, HALO, d_in), lambda s, t: (s, 0, 0)),
        ],
        out_shape=[
            jax.ShapeDtypeStruct((S, T, D), F32),
            jax.ShapeDtypeStruct((S, HALO, d_in), F32),
        ],
        scratch_shapes=[
            pltpu.VMEM((ns, d_in // LANES, HALO + ts, LANES), F32),
            pltpu.VMEM((ns * ts, d_in), F32),
            pltpu.VMEM((ns, ts, d_in), F32),
        ],
        compiler_params=_cparams(("arbitrary", "arbitrary")),
        name="conv_layer_carry" if carry else "conv_layer_state",
    )(x, state, w_in, b_in, dw_w, dw_b, cl_g, cl_b, w_out, ln_g, ln_b)
    return y, new_state


def _nsa_inproj_kernel(x_ref, w_ref, q_ref, kv_ref, sz_ref, g_ref, *, d_q, d_kv):
    xb = x_ref[...].astype(BF16)
    q_ref[...] = _dot(xb, w_ref[:, 0:d_q]).astype(BF16)
    kv_ref[...] = _dot(xb, w_ref[:, d_q:d_q + d_kv])
    z = _dot(xb, w_ref[:, d_q + d_kv:2 * d_q + d_kv])
    sz_ref[...] = z * _sigmoid(z)
    g_ref[...] = _sigmoid(_dot(xb, w_ref[:, 2 * d_q + d_kv:]))


def _nsa_inproj(x, w, *, d_q, d_kv, tm):
    N, D = x.shape
    d_g = w.shape[1] - 2 * d_q - d_kv
    row = lambda i: (i, 0)
    return pl.pallas_call(
        functools.partial(_nsa_inproj_kernel, d_q=d_q, d_kv=d_kv),
        grid=(N // tm,),
        in_specs=[pl.BlockSpec((tm, D), row), pl.BlockSpec(w.shape, lambda i: (0, 0))],
        out_specs=[pl.BlockSpec((tm, d_q), row), pl.BlockSpec((tm, d_kv), row),
                   pl.BlockSpec((tm, d_q), row), pl.BlockSpec((tm, d_g), row)],
        out_shape=[jax.ShapeDtypeStruct((N, d_q), BF16), jax.ShapeDtypeStruct((N, d_kv), F32),
                   jax.ShapeDtypeStruct((N, d_q), F32), jax.ShapeDtypeStruct((N, d_g), F32)],
        compiler_params=_cparams(("arbitrary",)),
        name="nsa_inproj",
    )(x, w)


def _nsa_inproj_prompt_kernel(x_ref, wn_ref, wt_ref, q_ref, sz_ref, g_ref, ksa_ref, kwa_ref, kvt_ref, va_ref,
                              *, d_q, kt_size):
    tm = x_ref.shape[1]
    t0 = pl.program_id(1) * tm
    xb = x_ref[0].astype(BF16)
    q_ref[0] = _dot(xb, wn_ref[:, 0:d_q]).astype(BF16)
    z = _dot(xb, wn_ref[:, d_q:2 * d_q])
    sz_ref[0] = z * _sigmoid(z)
    o = 2 * d_q
    g_ref[0] = _sigmoid(_dot(xb, wn_ref[:, o:o + LANES]))
    o += LANES
    kw = N_KV * K_AUG
    pos = t0 + lax.broadcasted_iota(jnp.int32, (tm, kw), 0)
    lane = lax.broadcasted_iota(jnp.int32, (tm, kw), 1) % K_AUG
    onehot = jnp.where(lane - HEAD_DIM == (pos % kt_size) // SEL_BLOCK, 1.0, 0.0)
    ksa_ref[0] = (_dot(xb, wn_ref[:, o:o + kw]) + onehot).astype(BF16)
    kwa_ref[0] = _dot(xb, wn_ref[:, o + kw:o + 2 * kw]).astype(BF16)
    hT = lax.dot_general(wt_ref[...], xb, (((1,), (1,)), ((), ())), preferred_element_type=F32)
    kvt_ref[0] = hT
    ones = jnp.concatenate([jnp.ones((1, tm), F32), jnp.zeros((V_AUG - HEAD_DIM - 1, tm), F32)], axis=0).astype(BF16)
    for i, st in enumerate((3, 5)):
        for g in range(N_KV):
            r0 = (st * N_KV + g) * HEAD_DIM
            va_ref[0, i, g, 0:HEAD_DIM, :] = hT[r0:r0 + HEAD_DIM].astype(BF16)
            va_ref[0, i, g, HEAD_DIM:V_AUG, :] = ones


def _nsa_inproj_prompt(x, w_nat, w_kvt, *, d_q, kt_size, tm):
    B, T, D = x.shape
    kw = N_KV * K_AUG
    tok = lambda n: pl.BlockSpec((1, tm, n), lambda b, t: (b, t, 0))
    const = lambda a: pl.BlockSpec(a.shape, lambda b, t: (0, 0))
    return pl.pallas_call(
        functools.partial(_nsa_inproj_prompt_kernel, d_q=d_q, kt_size=kt_size),
        grid=(B, T // tm),
        in_specs=[tok(D), const(w_nat), const(w_kvt)],
        out_specs=[tok(d_q), tok(d_q), tok(LANES), tok(kw), tok(kw),
                   pl.BlockSpec((1, w_kvt.shape[0], tm), lambda b, t: (b, 0, t)),
                   pl.BlockSpec((1, 2, N_KV, V_AUG, tm), lambda b, t: (b, 0, 0, 0, t))],
        out_shape=[jax.ShapeDtypeStruct((B, T, d_q), BF16), jax.ShapeDtypeStruct((B, T, d_q), F32),
                   jax.ShapeDtypeStruct((B, T, LANES), F32), jax.ShapeDtypeStruct((B, T, kw), BF16),
                   jax.ShapeDtypeStruct((B, T, kw), BF16), jax.ShapeDtypeStruct((B, w_kvt.shape[0], T), F32),
                   jax.ShapeDtypeStruct((B, 2, N_KV, V_AUG, T), BF16)],
        compiler_params=_cparams(("arbitrary", "arbitrary")),
        name="nsa_inproj_prompt",
    )(x, w_nat, w_kvt)


def _compress_finish(parts, base, w2):
    nchunk = parts.shape[0]
    second = pltpu.roll(parts[:, CMP_HID:], nchunk - 1, 0)
    row = lax.broadcasted_iota(jnp.int32, (nchunk, CMP_HID), 0)
    second = jnp.where(row < nchunk - 1, second, 0.0)
    h = base + parts[:, :CMP_HID] + second
    act = h * _sigmoid(h)
    return _dot(act.astype(BF16), w2).astype(BF16)


def _dot_f32_by_const(x, c, *, const_first):
    hi = x.astype(BF16)
    r1 = x - hi.astype(F32)
    mid = r1.astype(BF16)
    lo = (r1 - mid.astype(F32)).astype(BF16)
    out = None
    for part in (hi, mid, lo):
        d = _dot(c, part) if const_first else _dot(part, c)
        out = d if out is None else out + d
    return out


def _softmax_cols(s, valid):
    s = jnp.where(valid, s, NEG)
    m = jnp.max(s, axis=0, keepdims=True)
    e = jnp.where(valid, jnp.exp2(s - m), 0.0)
    return e / jnp.maximum(jnp.sum(e, axis=0, keepdims=True), 1e-30)


def _selection_scores(selimp, cur):
    j = lax.broadcasted_iota(jnp.int32, selimp.shape, 0)
    causal = j <= cur
    forced = (j == 0) | (j == cur) | (j == cur - 1)
    return jnp.where(forced, jnp.inf, jnp.where(causal, selimp, -jnp.inf)), causal


def _select_blocks(selimp, cur, n_select):
    ns = selimp.shape[0]
    j = lax.broadcasted_iota(jnp.int32, selimp.shape, 0)
    score, causal = _selection_scores(selimp, cur)
    taken = jnp.zeros(selimp.shape, jnp.int32)
    for _ in range(n_select):
        m = jnp.max(score, axis=0, keepdims=True)
        first = jnp.min(jnp.where(score == m, j, ns), axis=0, keepdims=True)
        hit = j == first
        taken = jnp.where(hit, 1, taken)
        score = jnp.where(hit, -jnp.inf, score)
    return (taken > 0) & causal


def _select_blocks_by_rank(selimp, cur, n_select, score_ref):
    ns = selimp.shape[0]
    score, causal = _selection_scores(selimp, cur)
    score_ref[...] = score
    sub = lax.broadcasted_iota(jnp.int32, (8, selimp.shape[1]), 0)
    ranks = []
    for v in range(ns // 8):
        blk = score[8 * v:8 * v + 8]
        rank = jnp.zeros(blk.shape, F32)
        for jp in range(ns):
            row = score_ref[jp:jp + 1, :]
            if jp < 8 * v:
                ahead = row >= blk
            elif jp >= 8 * v + 8:
                ahead = row > blk
            else:
                ahead = (row > blk) | ((row == blk) & (sub > jp - 8 * v))
            rank = rank + jnp.where(ahead, 1.0, 0.0)
        ranks.append(rank)
    return (jnp.concatenate(ranks, axis=0) < n_select) & causal


def _cmp_to_sel_matrix(n_sel, n_cmp):
    r = SEL_BLOCK // CMP_STRIDE
    lo = -(CMP_BLOCK // CMP_STRIDE - 1)
    m = np.zeros((n_sel, n_cmp), np.float32)
    for o in range(lo, r):
        start = o * CMP_STRIDE
        ov = max(0, min(SEL_BLOCK, start + CMP_BLOCK) - max(0, start))
        for jb in range(n_sel):
            n = r * jb + o
            if ov > 0 and 0 <= n < n_cmp:
                m[jb, n] += ov / CMP_BLOCK
    return m


V_AUG = 80
K_AUG = 128
BIAS_ROWS = 16


def _prompt_attn_kernel(q_ref, g_ref, kc_ref, vct_ref, ksa_ref, vsa_ref, kwa_ref, vwa_ref, mt_ref, o_ref,
                        bias_ref, score_ref, *, tq, kt_size, gps):
    qi = pl.program_id(1)
    qs = qi * tq
    hp = GROUP // 2
    RH = 2 * tq
    n_cmp = kc_ref.shape[2]
    n_sel = mt_ref.shape[0]
    blk_per_tile = kt_size // SEL_BLOCK
    chains = [(gi, h) for gi in range(gps) for h in range(hp)]

    t1 = qs + lax.broadcasted_iota(jnp.int32, (1, tq), 1)
    t_lane = jnp.concatenate([t1, t1], axis=1)
    pair = lambda gi, h: slice((gi * hp + h) * 2 * HEAD_DIM, (gi * hp + h + 1) * 2 * HEAD_DIM)
    qT = {}
    for gi, h in chains:
        t2 = q_ref[0, :, pair(gi, h)].astype(F32).T.astype(BF16)
        qT[gi, h] = jnp.concatenate([t2[0:HEAD_DIM], t2[HEAD_DIM:2 * HEAD_DIM]], axis=1)
    gT = g_ref[0].T
    k_lanes = lambda gi: slice(gi * K_AUG, (gi + 1) * K_AUG)

    wk = WINDOW + tq
    start = pl.multiple_of(jnp.maximum(qs - WINDOW, 0), 128)
    s_cmp = {c: _dot(kc_ref[0, c[0]], qT[c]) for c in chains}
    k_pad = jnp.zeros((K_AUG - HEAD_DIM, RH), BF16)
    s_win = {c: _dot(kwa_ref[0, pl.ds(start, wk), k_lanes(c[0])], jnp.concatenate([qT[c], k_pad], axis=0))
             for c in chains}

    n_idx = lax.broadcasted_iota(jnp.int32, (n_cmp, RH), 0)
    cmp_valid = n_idx * CMP_STRIDE + (CMP_BLOCK - 1) <= t_lane
    o_cmp = {}
    imps = [None] * gps
    for gi, h in chains:
        p_cmp = _softmax_cols(s_cmp[gi, h], cmp_valid)
        o_cmp[gi, h] = _dot(vct_ref[0, gi], p_cmp.astype(BF16))
        part = p_cmp[:, 0:tq] + p_cmp[:, tq:2 * tq]
        imps[gi] = part if imps[gi] is None else imps[gi] + part

    dpos = t_lane - (start + lax.broadcasted_iota(jnp.int32, (wk, RH), 0))
    win_valid = (dpos >= 0) & (dpos < WINDOW)
    o_win = {}
    for gi, h in chains:
        sw = jnp.where(win_valid, s_win[gi, h], NEG)
        p_win = jnp.exp2(sw - jnp.max(sw, axis=0, keepdims=True)).astype(BF16)
        acc_win = _dot(vwa_ref[0, 0, gi, :, pl.ds(start, wk)], p_win)
        o_win[gi, h] = acc_win[0:HEAD_DIM] / jnp.maximum(acc_win[HEAD_DIM:HEAD_DIM + 1], 1e-30)

    pad = jnp.zeros((BIAS_ROWS - blk_per_tile, tq), F32)
    for gi in range(gps):
        selimp = _dot_f32_by_const(imps[gi], mt_ref[...], const_first=True)
        sel = _select_blocks_by_rank(selimp, t1 // SEL_BLOCK, N_SELECT, score_ref.at[gi])
        bias = jnp.where(sel, 0.0, NEG)
        rows = []
        for kt in range(n_sel // blk_per_tile):
            rows += [bias[kt * blk_per_tile:(kt + 1) * blk_per_tile], pad]
        bias_rows = jnp.concatenate(rows, axis=0)
        bias_ref[gi] = jnp.concatenate([bias_rows, bias_rows], axis=1).astype(BF16)

    q_pad = jnp.zeros((K_AUG - HEAD_DIM - BIAS_ROWS, RH), BF16)

    def sel_units(kt, carries, diagonal):
        k0 = pl.multiple_of(kt * kt_size, kt_size)
        scores = []
        for gi, h in chains:
            b_t = bias_ref[gi, pl.ds(pl.multiple_of(kt * BIAS_ROWS, BIAS_ROWS), BIAS_ROWS), :]
            qa = jnp.concatenate([qT[gi, h], b_t, q_pad], axis=0)
            scores.append(_dot(ksa_ref[0, pl.ds(k0, kt_size), k_lanes(gi)], qa))
        out = []
        for (gi, h), s, (m, acc) in zip(chains, scores, carries):
            if diagonal:
                kidx = k0 + lax.broadcasted_iota(jnp.int32, (kt_size, RH), 0)
                s = jnp.where(kidx <= t_lane, s, NEG)
            m_new = jnp.maximum(m, jnp.max(s, axis=0, keepdims=True))
            p = jnp.exp2(s - m_new).astype(BF16)
            acc = jnp.exp2(m - m_new) * acc + _dot(vsa_ref[0, 0, gi, :, pl.ds(k0, kt_size)], p)
            out.append((m_new, acc))
        return tuple(out)

    n_full = qs // kt_size
    init = tuple((jnp.full((1, RH), NEG, F32), jnp.zeros((V_AUG, RH), F32)) for _ in chains)
    carries = lax.fori_loop(0, n_full, lambda kt, c: sel_units(kt, c, False), init)
    carries = sel_units(n_full, carries, True)

    def gate(gi, h, branch):
        r0 = branch * N_KV * GROUP + gi * GROUP + 2 * h
        return jnp.concatenate([gT[r0:r0 + 1], gT[r0 + 1:r0 + 2]], axis=1)

    for i, (gi, h) in enumerate(chains):
        acc_sel = carries[i][1]
        o_sel = acc_sel[0:HEAD_DIM] / jnp.maximum(acc_sel[HEAD_DIM:HEAD_DIM + 1], 1e-30)
        o = gate(gi, h, 0) * o_cmp[gi, h] + gate(gi, h, 1) * o_sel + gate(gi, h, 2) * o_win[gi, h]
        o_ref[0, :, pair(gi, h)] = jnp.concatenate([o[:, 0:tq], o[:, tq:2 * tq]], axis=0).T


def _prompt_attn(q, gates, kc, vcT, ksa, kwa, va, mT, *, tq, kt_size):
    B, T, att_w = q.shape
    G, NC, Dh = kc.shape[1:]
    n_sel = mT.shape[0]
    assert kt_size % SEL_BLOCK == 0 and kt_size // SEL_BLOCK <= BIAS_ROWS and T % kt_size == 0 and kt_size % tq == 0
    assert tq == LANES and G == N_KV
    whole = lambda a: pl.BlockSpec((1,) + a.shape[1:], lambda b, i: (b,) + (0,) * (a.ndim - 1))
    v_spec = lambda s: pl.BlockSpec((1, 1, G, V_AUG, T), lambda b, i: (b, s, 0, 0, 0))
    return pl.pallas_call(
        functools.partial(_prompt_attn_kernel, tq=tq, kt_size=kt_size, gps=G),
        grid=(B, T // tq),
        in_specs=[
            pl.BlockSpec((1, tq, att_w), lambda b, i: (b, i, 0)),
            pl.BlockSpec((1, tq, gates.shape[2]), lambda b, i: (b, i, 0)),
            whole(kc), whole(vcT), whole(ksa), v_spec(0), whole(kwa), v_spec(1),
            pl.BlockSpec(mT.shape, lambda b, i: (0, 0)),
        ],
        out_specs=pl.BlockSpec((1, tq, att_w), lambda b, i: (b, i, 0)),
        out_shape=jax.ShapeDtypeStruct((B, T, att_w), F32),
        scratch_shapes=[pltpu.VMEM((G, T // kt_size * BIAS_ROWS, 2 * tq), BF16),
                        pltpu.VMEM((G, n_sel, tq), F32)],
        compiler_params=_cparams(("arbitrary", "arbitrary")),
        name="prompt_attn",
    )(q, gates, kc, vcT, ksa, va, kwa, va, mT)


def _dot_tn(v, p):
    return lax.dot_general(v, p, (((0,), (0,)), ((), ())), preferred_element_type=F32)


PAGES_PER_STEP = 32
PANEL_PITCH = CMP_STRIDE + 1


def _page_specs(block_streams, stream_block, page):
    def spec(j):
        return pl.BlockSpec((1, block_streams, N_KV, HEAD_DIM, page),
                            lambda b, kt, pt: (pt[b, kt * PAGES_PER_STEP + j], stream_block, 0, 0, 0))
    return spec


def _sample_compress_kernel(pt_ref, *refs, page):
    del pt_ref
    pages = refs[:PAGES_PER_STEP]
    pe_ref, w1_ref, wp_ref, b1_ref, w2_ref, o_ref, panel, xs = refs[PAGES_PER_STEP:]
    kt = pl.program_id(1)
    cps = (page // CMP_STRIDE) * PAGES_PER_STEP
    row0 = pl.multiple_of(kt * cps, cps)
    for st in range(2):
        for gp in range(N_KV // 2):
            for j, p_ref in enumerate(pages):
                tile = p_ref[0, st, 2 * gp:2 * gp + 2].reshape(2 * HEAD_DIM, page).T
                for c in range(page // CMP_STRIDE):
                    r0 = (j * (page // CMP_STRIDE) + c) * PANEL_PITCH
                    panel[pl.ds(r0, CMP_STRIDE, stride=1), :] = tile[c * CMP_STRIDE:(c + 1) * CMP_STRIDE]
            for s in range(CMP_STRIDE):
                xs[st, gp, s, pl.ds(row0, cps), :] = panel[pl.ds(s, cps, stride=PANEL_PITCH), :].astype(BF16)

    @pl.when(kt == pl.num_programs(1) - 1)
    def _():
        for st in range(2):
            base = _dot(pe_ref[st], w1_ref[st])[0:1, :] + b1_ref[st]
            for gp in range(N_KV // 2):
                lhs = jnp.concatenate([xs[st, gp, s] for s in range(CMP_STRIDE)], axis=1)
                acc = _dot(lhs, wp_ref[st])
                for g2 in range(2):
                    parts = acc[:, g2 * 2 * CMP_HID:(g2 + 1) * 2 * CMP_HID]
                    o_ref[st, 0, 2 * gp + g2] = _compress_finish(parts, base, w2_ref[st])


def _sample_compress(cache_t, page_table, pe, w1, wp, b1, w2, *, name="sample_compress"):
    page = 2 * HEAD_DIM
    if page_table is None:
        DB, n_pages = cache_t.shape[0], cache_t.shape[-1] // page
        page_table = jnp.zeros((1, 1), jnp.int32)

        def spec(j):
            return pl.BlockSpec((1, 2, N_KV, HEAD_DIM, page), lambda b, kt, pt: (b, 0, 0, 0, kt * PAGES_PER_STEP + j))
    else:
        DB, n_pages = page_table.shape
        assert cache_t.shape[-1] == page
        spec = _page_specs(2, 0, page)
    nchunk = n_pages * page // CMP_STRIDE
    assert n_pages % PAGES_PER_STEP == 0 and page == LANES
    const = lambda a: pl.BlockSpec(a.shape, lambda b, kt, pt: (0,) * a.ndim)
    return pl.pallas_call(
        functools.partial(_sample_compress_kernel, page=page),
        grid_spec=pltpu.PrefetchScalarGridSpec(
            num_scalar_prefetch=1,
            grid=(DB, n_pages // PAGES_PER_STEP),
            in_specs=[spec(j) for j in range(PAGES_PER_STEP)] + [const(pe), const(w1), const(wp), const(b1), const(w2)],
            out_specs=pl.BlockSpec((2, 1, N_KV, nchunk, HEAD_DIM), lambda b, kt, pt: (0, b, 0, 0, 0)),
            scratch_shapes=[pltpu.VMEM((PAGES_PER_STEP * page // CMP_STRIDE * PANEL_PITCH, 2 * HEAD_DIM), F32),
                            pltpu.VMEM((2, N_KV // 2, CMP_STRIDE, nchunk, 2 * HEAD_DIM), BF16)],
        ),
        out_shape=jax.ShapeDtypeStruct((2, DB, N_KV, nchunk, HEAD_DIM), BF16),
        compiler_params=_cparams(("arbitrary", "arbitrary")),
        name=name,
    )(page_table, *([cache_t] * PAGES_PER_STEP), pe, w1, wp, b1, w2)


def _sample_attn_kernel(pt_ref, *refs, t_new, past, page):
    del pt_ref
    pages = refs[:PAGES_PER_STEP]
    (q_ref, g_ref, kc_ref, vc_ref, kn_ref, vn_ref, kw_ref, vw_ref, kwn_ref, vwn_ref, mt_ref, rr_ref,
     o_ref, bias_ref, m_ref, l_ref, acc_ref, oc_ref) = refs[PAGES_PER_STEP:]
    kt = pl.program_id(1)
    gd = N_KV * HEAD_DIM
    R = N_KV * GROUP * t_new
    qbd = q_ref[0]
    lane = lax.broadcasted_iota(jnp.int32, (1, R), 1)
    i_lane = lane % t_new
    t_lane = past + i_lane

    @pl.when(kt == 0)
    def _():
        n_cmp = kc_ref.shape[2]
        sc = None
        for g in range(N_KV):
            d = _dot(kc_ref[0, g], qbd[g * HEAD_DIM:(g + 1) * HEAD_DIM, :])
            sc = d if sc is None else sc + d
        n_idx = lax.broadcasted_iota(jnp.int32, (n_cmp, R), 0)
        p_cmp = _softmax_cols(sc, n_idx * CMP_STRIDE + (CMP_BLOCK - 1) <= t_lane)
        pb = p_cmp.astype(BF16)
        o_cmp = jnp.concatenate([_dot_tn(vc_ref[0, g], pb) for g in range(N_KV)], axis=0)

        imp = _dot_f32_by_const(p_cmp, rr_ref[...], const_first=False)
        selimp = _dot_f32_by_const(imp, mt_ref[...], const_first=True)
        sel = _select_blocks(selimp, t_lane // SEL_BLOCK, N_SELECT)
        bias_ref[...] = jnp.where(sel, 0.0, NEG)

        wbuf = kw_ref.shape[-1]
        sw = jnp.concatenate([_dot_tn(kw_ref[0].astype(BF16), qbd), _dot(kwn_ref[0], qbd)], axis=0)
        jw = lax.broadcasted_iota(jnp.int32, (wbuf + t_new, R), 0)
        dpos = wbuf + i_lane - jw
        p_win = _softmax_cols(sw, (dpos >= 0) & (dpos < WINDOW) & (past - wbuf + jw >= 0)).astype(BF16)
        o_win = _dot(vw_ref[0].astype(BF16), p_win[0:wbuf]) + _dot_tn(vwn_ref[0], p_win[wbuf:wbuf + t_new])
        oc_ref[...] = g_ref[0, 0:1, :] * o_cmp + g_ref[0, 2:3, :] * o_win

        new_blk = past // SEL_BLOCK
        s_new = _dot(kn_ref[0], qbd) + bias_ref[new_blk:new_blk + 1, :]
        ip = lax.broadcasted_iota(jnp.int32, (t_new, R), 0)
        s_new = jnp.where(ip <= i_lane, s_new, NEG)
        m0 = jnp.max(s_new, axis=0, keepdims=True)
        p0 = jnp.exp2(s_new - m0)
        m_ref[0] = m0
        l_ref[0] = jnp.sum(p0, axis=0, keepdims=True)
        acc_ref[0] = _dot_tn(vn_ref[0], p0.astype(BF16))
        m_ref[1] = jnp.full((1, R), NEG, F32)
        l_ref[1] = jnp.zeros((1, R), F32)
        acc_ref[1] = jnp.zeros((gd, R), F32)

    blk_per_page = page // SEL_BLOCK
    nb = PAGES_PER_STEP * blk_per_page
    b_t = bias_ref[pl.ds(pl.multiple_of(kt * nb, nb), nb), :]
    ppc = PAGES_PER_STEP // 2
    kpc = ppc * page
    scores = []
    for c in range(2):
        kT = jnp.concatenate([p_ref[0, 0].reshape(gd, page).astype(BF16) for p_ref in pages[c * ppc:(c + 1) * ppc]], axis=1)
        scores.append(jnp.concatenate([_dot_tn(kT[:, 0:kpc // 2], qbd), _dot_tn(kT[:, kpc // 2:], qbd)], axis=0))
    state = []
    for c in range(2):
        vT = jnp.concatenate([p_ref[0, 1].reshape(gd, page).astype(BF16) for p_ref in pages[c * ppc:(c + 1) * ppc]], axis=1)
        nbc = nb // 2
        s = jnp.concatenate([scores[c][jb * SEL_BLOCK:(jb + 1) * SEL_BLOCK] + b_t[c * nbc + jb:c * nbc + jb + 1, :]
                             for jb in range(nbc)], axis=0)
        m_old = m_ref[c]
        m_new = jnp.maximum(m_old, jnp.max(s, axis=0, keepdims=True))
        alpha = jnp.exp2(m_old - m_new)
        p = jnp.exp2(s - m_new)
        l = alpha * l_ref[c] + jnp.sum(p, axis=0, keepdims=True)
        pb = p.astype(BF16)
        acc = alpha * acc_ref[c] + jnp.concatenate([_dot(vT[0:gd // 2], pb), _dot(vT[gd // 2:], pb)], axis=0)
        m_ref[c] = m_new
        l_ref[c] = l
        acc_ref[c] = acc
        state.append((m_new, l, acc))

    @pl.when(kt == pl.num_programs(1) - 1)
    def _():
        (m_a, l_a, acc_a), (m_b, l_b, acc_b) = state
        m = jnp.maximum(m_a, m_b)
        w_a, w_b = jnp.exp2(m_a - m), jnp.exp2(m_b - m)
        l = w_a * l_a + w_b * l_b
        acc = w_a * acc_a + w_b * acc_b
        o = oc_ref[...] + g_ref[0, 1:2, :] * (acc / jnp.maximum(l, 1e-30))
        grp = lane // (GROUP * t_new)
        out = jnp.zeros((HEAD_DIM, R), F32)
        for g in range(N_KV):
            out = out + jnp.where(grp == g, o[g * HEAD_DIM:(g + 1) * HEAD_DIM, :], 0.0)
        o_ref[0] = out


def _sample_attn(cache_t, page_table, qbd, gT, kcv, kn, vn, kw, vw, kwn, vwn, mT, rr, *, t_new):
    page = cache_t.shape[-1]
    DB, n_pages = page_table.shape
    past = n_pages * page
    R = qbd.shape[-1]
    per_b = lambda a: pl.BlockSpec((1,) + a.shape[1:], lambda b, kt, pt: (b,) + (0,) * (a.ndim - 1))
    const = lambda a: pl.BlockSpec(a.shape, lambda b, kt, pt: (0,) * a.ndim)
    cmp_spec = lambda st: pl.BlockSpec((None, 1) + kcv.shape[2:], lambda b, kt, pt: (st, b, 0, 0, 0))
    spec = _page_specs(2, 1, page)
    return pl.pallas_call(
        functools.partial(_sample_attn_kernel, t_new=t_new, past=past, page=page),
        grid_spec=pltpu.PrefetchScalarGridSpec(
            num_scalar_prefetch=1,
            grid=(DB, n_pages // PAGES_PER_STEP),
            in_specs=[spec(j) for j in range(PAGES_PER_STEP)]
            + [per_b(qbd), per_b(gT), cmp_spec(0), cmp_spec(1), per_b(kn), per_b(vn), per_b(kw), per_b(vw),
               per_b(kwn), per_b(vwn), const(mT), const(rr)],
            out_specs=pl.BlockSpec((1, HEAD_DIM, R), lambda b, kt, pt: (b, 0, 0)),
            scratch_shapes=[pltpu.VMEM((mT.shape[0], R), F32), pltpu.VMEM((2, 1, R), F32), pltpu.VMEM((2, 1, R), F32),
                            pltpu.VMEM((2, N_KV * HEAD_DIM, R), F32), pltpu.VMEM((N_KV * HEAD_DIM, R), F32)],
        ),
        out_shape=jax.ShapeDtypeStruct((DB, HEAD_DIM, R), F32),
        compiler_params=_cparams(("arbitrary", "arbitrary")),
        name="sample_attn",
    )(page_table, *([cache_t] * PAGES_PER_STEP), qbd, gT, kcv, kcv, kn, vn, kw, vw, kwn, vwn, mT, rr)


def _nsa_outproj_kernel(o_ref, sz_ref, x_ref, w_ref, g_ref, b_ref, y_ref):
    gated = (o_ref[...] * sz_ref[...]).astype(BF16)
    y = _dot(gated, w_ref[...])
    y_ref[...] = _layer_norm(ALPHA * x_ref[...] + y, g_ref[...], b_ref[...])


def _nsa_outproj(o, sz, x, w, g, b, *, tm):
    N, D = x.shape
    row = lambda i: (i, 0)
    const = lambda i: (0, 0)
    return pl.pallas_call(
        _nsa_outproj_kernel,
        grid=(N // tm,),
        in_specs=[pl.BlockSpec((tm, o.shape[1]), row), pl.BlockSpec((tm, sz.shape[1]), row),
                  pl.BlockSpec((tm, D), row), pl.BlockSpec(w.shape, const),
                  pl.BlockSpec(g.shape, const), pl.BlockSpec(b.shape, const)],
        out_specs=pl.BlockSpec((tm, D), row),
        out_shape=jax.ShapeDtypeStruct((N, D), F32),
        compiler_params=_cparams(("arbitrary",)),
        name="nsa_outproj",
    )(o, sz, x, w, g, b)


def _prep_nsa_weights(w_in, pe, w1, b1, w2):
    D = w_in.shape[0]
    att_w = N_KV * GROUP * HEAD_DIM
    kv_w = 6 * N_KV * HEAD_DIM
    o1, o2 = att_w, att_w + kv_w
    o3 = o2 + 3 * N_KV * GROUP
    w_q = w_in[:, :o1] * (HEAD_DIM ** -0.5 * np.log2(np.e))
    w_g = w_in[:, o2:o3].reshape(D, N_KV, GROUP, 3).transpose(0, 3, 1, 2).reshape(D, 3 * N_KV * GROUP)
    w_g = jnp.pad(w_g, ((0, 0), (0, 128 - w_g.shape[1])))
    w_all = jnp.concatenate([w_q, w_in[:, o1:o2], w_in[:, o3:], w_g], axis=1).astype(BF16)
    kv_cols = w_in[:, o1:o2].reshape(D, 6, N_KV, HEAD_DIM)
    padded = lambda st: jnp.pad(kv_cols[:, st], ((0, 0), (0, 0), (0, K_AUG - HEAD_DIM))).reshape(D, N_KV * K_AUG)
    w_nat = jnp.concatenate([w_q, w_in[:, o3:], w_g, padded(2), padded(4)], axis=1).astype(BF16)
    w_kvt = w_in[:, o1:o2].T.astype(BF16)
    r = CMP_BLOCK // CMP_STRIDE
    pe_rows = jnp.pad(pe.reshape(2, 1, CMP_BLOCK * HEAD_DIM), ((0, 0), (0, 7), (0, 0))).astype(BF16)
    w1_flat = w1.reshape(2, CMP_BLOCK * HEAD_DIM, CMP_HID).astype(BF16)
    w1p = w1.reshape(2, r, CMP_STRIDE, HEAD_DIM, CMP_HID).transpose(0, 2, 3, 1, 4)
    w1p = w1p.reshape(2, CMP_STRIDE, HEAD_DIM, r * CMP_HID)
    eye = jnp.eye(2, dtype=w1.dtype)
    w1_bd = w1p[:, :, None, :, None, :] * eye[None, None, :, None, :, None]
    w1_bd = w1_bd.reshape(2, CMP_STRIDE * 2 * HEAD_DIM, 2 * r * CMP_HID).astype(BF16)
    return (w_all, w_nat, w_kvt), pe_rows, w1_flat, w1_bd, b1.reshape(2, 1, CMP_HID), w2.astype(BF16)


def _nsa_prompt(x1, nsa_w, w_out, g, b):
    B, T, D = x1.shape
    (_, w_nat, w_kvt), pe_rows, w1_flat, w1_bd, b1, w2 = nsa_w
    att_w = N_KV * GROUP * HEAD_DIM
    kt_size = 512
    q, sz, gates, ksa, kwa, kvt, va = _nsa_inproj_prompt(x1, w_nat, w_kvt, d_q=att_w, kt_size=kt_size, tm=512)
    kvt6 = kvt.reshape(B, 6, N_KV, HEAD_DIM, T)
    rows = kvt6[:, :4].transpose(0, 4, 1, 2, 3)
    win = kvt6[:, 4:, :, :, T - min(WINDOW, T):].transpose(0, 4, 1, 2, 3)
    nchunk = T // CMP_STRIDE
    kcv = _sample_compress(kvt6, None, pe_rows, w1_flat, w1_bd, b1, w2, name="prompt_compress")
    mT = jnp.asarray(_cmp_to_sel_matrix(T // SEL_BLOCK, nchunk), BF16)
    o = _prompt_attn(q, gates, kcv[0], kcv[1].transpose(0, 1, 3, 2), ksa, kwa, va, mT, tq=128, kt_size=kt_size)
    y = _nsa_outproj(o.reshape(B * T, att_w), sz.reshape(B * T, att_w), x1.reshape(B * T, D), w_out, g, b, tm=512)
    return y.reshape(B, T, D), rows, win


def _nsa_sample(x1, cache_kv_l, cache_win_l, page_table, nsa_w, w_out, g, b):
    DB, T, D = x1.shape
    (w_all, _, _), pe_rows, w1_flat, w1_bd, b1, w2 = nsa_w
    att_w = N_KV * GROUP * HEAD_DIM
    kv_w = 6 * N_KV * HEAD_DIM
    page = cache_kv_l.shape[1]
    past = page_table.shape[1] * page
    assert T <= CMP_STRIDE and past % SEL_BLOCK == 0 and past % CMP_STRIDE == 0
    x_flat = x1.reshape(DB * T, D)
    q, kv, sz, gates = _nsa_inproj(x_flat, w_all, d_q=att_w, d_kv=kv_w, tm=DB * T)
    kv6 = kv.reshape(DB, T, 6, N_KV, HEAD_DIM)
    rows = kv6[:, :, :4]
    wseq = jnp.concatenate([cache_win_l, kv6[:, :, 4:]], axis=1)
    gd = N_KV * HEAD_DIM
    R = N_KV * GROUP * T
    qT = q.reshape(DB, T, N_KV, GROUP, HEAD_DIM).transpose(0, 2, 4, 3, 1)
    eye = jnp.eye(N_KV, dtype=BF16)
    qbd = (qT[:, :, :, None] * eye[None, :, None, :, None, None]).reshape(DB, gd, R)
    gT = gates[:, :3 * N_KV * GROUP].reshape(DB, T, 3, N_KV * GROUP).transpose(0, 2, 3, 1).reshape(DB, 3, R)
    gT = jnp.pad(gT, ((0, 0), (0, 5), (0, 0)))
    new_rows = lambda st: kv6[:, :, st].reshape(DB, T, gd).astype(BF16)
    cache_t = cache_kv_l.transpose(0, 2, 3, 4, 1)
    win_t = cache_win_l.transpose(0, 2, 3, 4, 1)
    wbuf = win_t.shape[-1]
    nchunk = past // CMP_STRIDE
    kcv = _sample_compress(cache_t, page_table, pe_rows, w1_flat, w1_bd, b1, w2)
    n_sel = past // SEL_BLOCK + 1
    n_sel_pad = -(-n_sel // (PAGES_PER_STEP * page // SEL_BLOCK)) * (PAGES_PER_STEP * page // SEL_BLOCK)
    mT = np.zeros((n_sel_pad, nchunk), np.float32)
    mT[:n_sel] = _cmp_to_sel_matrix(n_sel, nchunk)
    lane = np.arange(R)
    rr = ((lane[:, None] % T == lane[None, :] % T)
          & (lane[:, None] // (GROUP * T) == lane[None, :] // (GROUP * T))).astype(np.float32)
    oT = _sample_attn(cache_t, page_table, qbd, gT, kcv, new_rows(2), new_rows(3),
                      win_t[:, 0].reshape(DB, gd, wbuf), win_t[:, 1].reshape(DB, gd, wbuf), new_rows(4), new_rows(5),
                      jnp.asarray(mT, BF16), jnp.asarray(rr, BF16), t_new=T)
    o = oT.reshape(DB, HEAD_DIM, N_KV, GROUP, T).transpose(0, 4, 2, 3, 1).reshape(DB * T, att_w)
    y = _nsa_outproj(o, sz, x_flat, w_out, g, b, tm=DB * T)
    return y.reshape(DB, T, D), rows, wseq[:, T:]


def kernel(x_prompt, x_sample, state_conv, cache_kv, cache_win, page_table, conv_w_in, conv_b_in, conv_dw_w,
           conv_dw_b, conv_ln_g, conv_ln_b, conv_w_out, nsa_w_in, cmp_pe, cmp_w1, cmp_b1, cmp_w2, nsa_w_out,
           ln_g, ln_b):
    B, T, D = x_prompt.shape
    DB, TS, _ = x_sample.shape
    d_in = conv_w_out.shape[1]
    row = lambda v: v.reshape(1, -1)

    cw = (conv_w_in[0].astype(BF16), row(conv_b_in[0]), conv_dw_w[0], row(conv_dw_b[0]), row(conv_ln_g[0]),
          row(conv_ln_b[0]), conv_w_out[0].astype(BF16), row(ln_g[0]), row(ln_b[0]))
    pad = HALO - (CONV_W - 1)
    xp, sp = _conv_layer(x_prompt, jnp.zeros((B, HALO, d_in), F32), *cw, ts=256, carry=True)
    st = jnp.pad(state_conv[0], ((0, 0), (pad, 0), (0, 0)))
    xs, ss = _conv_layer(x_sample, st, *cw, ts=TS, carry=False)

    nsa_w = _prep_nsa_weights(nsa_w_in[0], cmp_pe[0], cmp_w1[0], cmp_b1[0], cmp_w2[0])
    w_out = nsa_w_out[0].astype(BF16)
    yp, rp, wp = _nsa_prompt(xp, nsa_w, w_out, row(ln_g[1]), row(ln_b[1]))
    ys, rs, ws = _nsa_sample(xs, cache_kv[0], cache_win[0], page_table, nsa_w, w_out, row(ln_g[1]), row(ln_b[1]))
    return (yp, ys, sp[None, :, pad:], ss[None, :, pad:], rp[None], rs[None], wp[None], ws[None])
```
